```python
import jax, jax.numpy as jnp
from jax import lax
import numpy as np

D_MODEL = 2048
BATCH = 4
SEQ = 4096
DEPTH = 2
DEC_BATCH = 8
DEC_SEQ = 2048
PAST_LEN = 128

N_HEADS = 16
QK_NOPE_DIM = 128
QK_ROPE_DIM = 64
V_HEAD_DIM = 128
Q_LORA_RANK = D_MODEL // 4
KV_LORA_RANK = D_MODEL // 4
ROPE_THETA = 10000.0
Q_BLOCK = 128
FOURIER_WIDTH = D_MODEL // 2
N_FOURIER_GROUPS = 4
FOURIER_GROUP_DIM = FOURIER_WIDTH // N_FOURIER_GROUPS
N_BRANCHES = 2
D_FF = 11 * D_MODEL // 4
NORM_EPS = 1e-6
SPLITS = (FOURIER_WIDTH,
          FOURIER_WIDTH + Q_LORA_RANK,
          FOURIER_WIDTH + Q_LORA_RANK + KV_LORA_RANK,
          FOURIER_WIDTH + Q_LORA_RANK + KV_LORA_RANK + QK_ROPE_DIM)
D_IN = SPLITS[-1] + N_BRANCHES * D_MODEL

kernel_name = "hybrid_fnet_mla_macaron_encoder"


def rmsnorm(x, g):
    xf = x.astype(jnp.float32)
    y = xf * lax.rsqrt(jnp.mean(xf * xf, axis=-1, keepdims=True) + NORM_EPS)
    return y.astype(x.dtype) * g


def swiglu(h, w_gate, w_up, w_down):
    return (jax.nn.silu(h @ w_gate) * (h @ w_up)) @ w_down


def rope_tables(seq):
    half = QK_ROPE_DIM // 2
    inv_freq = 1.0 / (ROPE_THETA ** (jnp.arange(half, dtype=jnp.float32) / half))
    ang = jnp.arange(seq, dtype=jnp.float32)[:, None] * inv_freq[None, :]
    return jnp.cos(ang), jnp.sin(ang)


def apply_rope(x, cos, sin):
    x1, x2 = jnp.split(x.astype(jnp.float32), 2, axis=-1)
    out = jnp.concatenate([x1 * cos - x2 * sin, x1 * sin + x2 * cos], axis=-1)
    return out.astype(x.dtype)


def mla_attention(q_nope, q_rope, k_nope, k_rope, v):
    b, s, h, _ = q_nope.shape
    nblk = s // Q_BLOCK
    scale = (QK_NOPE_DIM + QK_ROPE_DIM) ** -0.5

    def to_blocks(t):
        return jnp.moveaxis(t.reshape(b, nblk, Q_BLOCK, *t.shape[2:]), 1, 0)

    def block(qs):
        qn, qr = qs
        sc = (jnp.einsum('bqhd,bkhd->bhqk', qn, k_nope)
              + jnp.einsum('bqhr,bkr->bhqk', qr, k_rope))
        p = jax.nn.softmax(sc.astype(jnp.float32) * scale, axis=-1).astype(v.dtype)
        return jnp.einsum('bhqk,bkhd->bqhd', p, v)

    out = lax.map(block, (to_blocks(q_nope), to_blocks(q_rope)))
    return jnp.moveaxis(out, 0, 1).reshape(b, s, h * V_HEAD_DIM)


def token_mixer(h, w_in, b_gate, q_a_norm, kv_a_norm, w_uq, w_ukv, w_fourier, w_mla_o, w_out, cos, sin):
    b, s, _ = h.shape
    proj = h @ w_in
    u_f, c_q, c_kv, k_r, gate_pre = jnp.split(proj, SPLITS, axis=-1)

    uf = u_f.reshape(b, s, N_FOURIER_GROUPS, FOURIER_GROUP_DIM).astype(jnp.float32)
    mixed = jnp.fft.fft2(uf, axes=(1, 3), norm="ortho").real.astype(h.dtype)
    y_a = mixed.reshape(b, s, FOURIER_WIDTH) @ w_fourier

    q = (rmsnorm(c_q, q_a_norm) @ w_uq).reshape(b, s, N_HEADS, QK_NOPE_DIM + QK_ROPE_DIM)
    q_nope, q_rope = jnp.split(q, [QK_NOPE_DIM], axis=-1)
    kv = (rmsnorm(c_kv, kv_a_norm) @ w_ukv).reshape(b, s, N_HEADS, QK_NOPE_DIM + V_HEAD_DIM)
    k_nope, v = jnp.split(kv, [QK_NOPE_DIM], axis=-1)
    q_rope = apply_rope(q_rope, cos[:, None, :], sin[:, None, :])
    k_rope = apply_rope(k_r, cos, sin)
    y_b = mla_attention(q_nope, q_rope, k_nope, k_rope, v) @ w_mla_o

    gates = jax.nn.sigmoid((gate_pre + b_gate).astype(jnp.float32)).astype(h.dtype)
    g_a, g_b = jnp.split(gates, N_BRANCHES, axis=-1)
    return (g_a * y_a + g_b * y_b) @ w_out


def trunk(x, ffn1_norm, ffn1_w_gate, ffn1_w_up, ffn1_w_down, mix_norm, w_in, b_gate,
          q_a_norm, kv_a_norm, w_uq, w_ukv, w_fourier, w_mla_o, w_out,
          ffn2_norm, ffn2_w_gate, ffn2_w_up, ffn2_w_down, final_norm):
    cos, sin = rope_tables(x.shape[1])
    for l in range(DEPTH):
        x = x + 0.5 * swiglu(rmsnorm(x, ffn1_norm[l]), ffn1_w_gate[l], ffn1_w_up[l], ffn1_w_down[l])
        x = x + token_mixer(rmsnorm(x, mix_norm[l]), w_in[l], b_gate[l], q_a_norm[l], kv_a_norm[l],
                            w_uq[l], w_ukv[l], w_fourier[l], w_mla_o[l], w_out[l], cos, sin)
        x = x + 0.5 * swiglu(rmsnorm(x, ffn2_norm[l]), ffn2_w_gate[l], ffn2_w_up[l], ffn2_w_down[l])
    return rmsnorm(x, final_norm)


def setup_inputs(seed: int = 0) -> dict:
    key = jax.random.key(seed)
    ks = jax.random.split(key, 24)

    def w(k, shape, fan_in):
        return jax.random.normal(k, shape, jnp.float32) * (fan_in ** -0.5)

    def gain(k, shape):
        return 1.0 + 0.02 * jax.random.normal(k, shape, jnp.float32)

    L = DEPTH
    return {
        "x_prompt": jax.random.normal(ks[0], (BATCH, SEQ, D_MODEL), jnp.float32),
        "x_sample": jax.random.normal(ks[1], (DEC_BATCH, DEC_SEQ, D_MODEL), jnp.float32),
        "ffn1_norm": gain(ks[2], (L, D_MODEL)),
        "ffn1_w_gate": w(ks[3], (L, D_MODEL, D_FF), D_MODEL),
        "ffn1_w_up": w(ks[4], (L, D_MODEL, D_FF), D_MODEL),
        "ffn1_w_down": w(ks[5], (L, D_FF, D_MODEL), D_FF),
        "mix_norm": gain(ks[6], (L, D_MODEL)),
        "w_in": w(ks[7], (L, D_MODEL, D_IN), D_MODEL),
        "b_gate": 0.01 * jax.random.normal(ks[8], (L, N_BRANCHES * D_MODEL), jnp.float32),
        "q_a_norm": gain(ks[9], (L, Q_LORA_RANK)),
        "kv_a_norm": gain(ks[10], (L, KV_LORA_RANK)),
        "w_uq": w(ks[11], (L, Q_LORA_RANK, N_HEADS * (QK_NOPE_DIM + QK_ROPE_DIM)), Q_LORA_RANK),
        "w_ukv": w(ks[12], (L, KV_LORA_RANK, N_HEADS * (QK_NOPE_DIM + V_HEAD_DIM)), KV_LORA_RANK),
        "w_fourier": w(ks[13], (L, FOURIER_WIDTH, D_MODEL), FOURIER_WIDTH),
        "w_mla_o": w(ks[14], (L, N_HEADS * V_HEAD_DIM, D_MODEL), N_HEADS * V_HEAD_DIM),
        "w_out": w(ks[15], (L, D_MODEL, D_MODEL), D_MODEL),
        "ffn2_norm": gain(ks[16], (L, D_MODEL)),
        "ffn2_w_gate": w(ks[17], (L, D_MODEL, D_FF), D_MODEL),
        "ffn2_w_up": w(ks[18], (L, D_MODEL, D_FF), D_MODEL),
        "ffn2_w_down": w(ks[19], (L, D_FF, D_MODEL), D_FF),
        "final_norm": gain(ks[20], (D_MODEL,)),
    }


def reference(x_prompt, x_sample, ffn1_norm, ffn1_w_gate, ffn1_w_up, ffn1_w_down, mix_norm, w_in, b_gate,
              q_a_norm, kv_a_norm, w_uq, w_ukv, w_fourier, w_mla_o, w_out,
              ffn2_norm, ffn2_w_gate, ffn2_w_up, ffn2_w_down, final_norm):
    y_prompt = trunk(x_prompt, ffn1_norm, ffn1_w_gate, ffn1_w_up, ffn1_w_down, mix_norm, w_in, b_gate,
                     q_a_norm, kv_a_norm, w_uq, w_ukv, w_fourier, w_mla_o, w_out,
                     ffn2_norm, ffn2_w_gate, ffn2_w_up, ffn2_w_down, final_norm)
    y_sample = trunk(x_sample, ffn1_norm, ffn1_w_gate, ffn1_w_up, ffn1_w_down, mix_norm, w_in, b_gate,
                     q_a_norm, kv_a_norm, w_uq, w_ukv, w_fourier, w_mla_o, w_out,
                     ffn2_norm, ffn2_w_gate, ffn2_w_up, ffn2_w_down, final_norm)
    return (y_prompt, y_sample)
```

```python
import functools
import math

import jax
import jax.numpy as jnp
from jax import lax
from jax.experimental import pallas as pl
from jax.experimental.pallas import tpu as pltpu

N_HEADS = 16
QK_NOPE_DIM = 128
QK_ROPE_DIM = 64
V_HEAD_DIM = 128
N_FOURIER_GROUPS = 4
N_BRANCHES = 2
ROPE_THETA = 10000.0
NORM_EPS = 1e-6

LANES = 128
QK_PAD_DIM = 2 * LANES
VMEM_LIMIT_BYTES = 56 * 1024 * 1024

F32 = jnp.float32
BF16 = jnp.bfloat16


def _params(*semantics):
    return pltpu.CompilerParams(dimension_semantics=semantics, vmem_limit_bytes=VMEM_LIMIT_BYTES)


def _resident(shape):
    zeros = (0,) * len(shape)
    return pl.BlockSpec(shape, lambda *_: zeros, pipeline_mode=pl.Buffered(1))


def _rms(x):
    return x * lax.rsqrt(jnp.mean(x * x, axis=-1, keepdims=True) + NORM_EPS)


def _dot(a, b):
    return jnp.dot(a, b, preferred_element_type=F32)


def _ffn_kernel(x_ref, g_ref, wg_ref, wu_ref, wd_ref, *rest, n_ff, final):
    if final:
        fg_ref, o_ref, h_ref = rest
    else:
        o_ref, h_ref = rest
    f = pl.program_id(1)

    @pl.when(f == 0)
    def _():
        x = x_ref[...]
        h_ref[...] = (_rms(x) * g_ref[...]).astype(BF16)
        o_ref[...] = x

    h = h_ref[...]
    a = _dot(h, wg_ref[...])
    u = _dot(h, wu_ref[...])
    act = (0.5 * (a * jax.nn.sigmoid(a)) * u).astype(BF16)
    o_ref[...] += _dot(act, wd_ref[...])

    if final:
        @pl.when(f == n_ff - 1)
        def _():
            o_ref[...] = _rms(o_ref[...]) * fg_ref[...]


def _ffn(x, g, wg, wu, wd, final_g=None, *, tm=512, tf=512):
    t, d = x.shape
    d_ff = wg.shape[1]
    tm, tf = min(tm, t), min(tf, d_ff)
    n_ff = d_ff // tf
    assert t % tm == 0 and d_ff % tf == 0
    final = final_g is not None
    in_specs = [
        pl.BlockSpec((tm, d), lambda i, f: (i, 0)),
        _resident((1, d)),
        pl.BlockSpec((d, tf), lambda i, f: (0, f)),
        pl.BlockSpec((d, tf), lambda i, f: (0, f)),
        pl.BlockSpec((tf, d), lambda i, f: (f, 0)),
    ]
    args = [x, g, wg, wu, wd]
    if final:
        in_specs.append(_resident((1, d)))
        args.append(final_g)
    return pl.pallas_call(
        functools.partial(_ffn_kernel, n_ff=n_ff, final=final),
        out_shape=jax.ShapeDtypeStruct((t, d), F32),
        grid=(t // tm, n_ff),
        in_specs=in_specs,
        out_specs=pl.BlockSpec((tm, d), lambda i, f: (i, 0)),
        scratch_shapes=[pltpu.VMEM((tm, d), BF16)],
        compiler_params=_params("parallel", "arbitrary"),
        name="ffn_final" if final else "ffn",
    )(*args)


def _fourier_in_kernel(x_ref, g_ref, wf_ref, cs_ref, o_ref, *, gd, n_groups):
    h = (_rms(x_ref[...]) * g_ref[...]).astype(BF16)
    u = _dot(h, wf_ref[...]).astype(BF16)
    cs = cs_ref[...]
    for grp in range(n_groups):
        cols = slice(grp * gd, (grp + 1) * gd)
        a = _dot(u[:, cols], cs)
        o_ref[0, :, cols] = a[:, :gd].astype(BF16)
        o_ref[1, :, cols] = a[:, gd:].astype(BF16)


def _fourier_in(x, g, wf, cs, batch, seq, *, tm=512):
    t, d = x.shape
    fw = wf.shape[1]
    gd = fw // N_FOURIER_GROUPS
    tm = min(tm, seq)
    tps = seq // tm
    assert seq % tm == 0
    return pl.pallas_call(
        functools.partial(_fourier_in_kernel, gd=gd, n_groups=N_FOURIER_GROUPS),
        out_shape=jax.ShapeDtypeStruct((batch, 2, seq, fw), BF16),
        grid=(t // tm,),
        in_specs=[
            pl.BlockSpec((tm, d), lambda i: (i, 0)),
            _resident((1, d)),
            _resident((d, fw)),
            _resident((gd, 2 * gd)),
        ],
        out_specs=pl.BlockSpec((None, 2, tm, fw), lambda i: (i // tps, 0, i % tps, 0)),
        compiler_params=_params("parallel"),
        name="fourier_in",
    )(x, g, wf, cs)


def _seq_dft_kernel(f_ref, r_ref, o_ref):
    o_ref[...] = _dot(f_ref[...], r_ref[...]).astype(BF16)


def _seq_dft(fmat, r, *, tm=512, tn=512):
    batch, s2, fw = r.shape
    seq = s2 // 2
    tm, tn = min(tm, seq), min(tn, fw)
    tps = seq // tm
    assert seq % tm == 0 and fw % tn == 0
    return pl.pallas_call(
        _seq_dft_kernel,
        out_shape=jax.ShapeDtypeStruct((batch * seq, fw), BF16),
        grid=(batch, fw // tn, tps),
        in_specs=[
            pl.BlockSpec((tm, s2), lambda b, n, i: (i, 0)),
            pl.BlockSpec((None, s2, tn), lambda b, n, i: (b, 0, n)),
        ],
        out_specs=pl.BlockSpec((tm, tn), lambda b, n, i: (b * tps + i, n)),
        compiler_params=_params("parallel", "parallel", "arbitrary"),
        name="seq_dft",
    )(fmat, r)


def _latent_kernel(x_ref, g_ref, wlat_ref, qg_ref, kvg_ref, wqn_ref, wqr_ref, wqrr_ref, wk_ref, wv_ref,
                   cos_ref, sin_ref, q_ref, k_ref, v_ref, *, ql, kl, n_heads, scale):
    h = (_rms(x_ref[...]) * g_ref[...]).astype(BF16)
    lat = _dot(h, wlat_ref[...])
    cq = (_rms(lat[:, :ql]) * qg_ref[...]).astype(BF16)
    ckv = (_rms(lat[:, ql:ql + kl]) * kvg_ref[...]).astype(BF16)
    cos = cos_ref[...]
    sin = sin_ref[...]
    kr = lat[:, ql + kl:ql + kl + LANES]
    krr = lat[:, ql + kl + LANES:ql + kl + 2 * LANES]
    k_rope = (kr * cos + krr * sin).astype(BF16)
    qn = _dot(cq, wqn_ref[...]) * scale
    qa = _dot(cq, wqr_ref[...])
    qb = _dot(cq, wqrr_ref[...])
    kn = _dot(ckv, wk_ref[...]).astype(BF16)
    v_ref[...] = _dot(ckv, wv_ref[...]).astype(BF16)
    for hd in range(n_heads):
        src = slice(hd * LANES, (hd + 1) * LANES)
        lo = slice(hd * QK_PAD_DIM, hd * QK_PAD_DIM + LANES)
        hi = slice(hd * QK_PAD_DIM + LANES, (hd + 1) * QK_PAD_DIM)
        q_ref[:, lo] = qn[:, src].astype(BF16)
        q_ref[:, hi] = ((qa[:, src] * cos + qb[:, src] * sin) * scale).astype(BF16)
        k_ref[:, lo] = kn[:, src]
        k_ref[:, hi] = k_rope


def _latent(x, g, wlat, qg, kvg, wqn, wqr, wqrr, wk, wv, cos, sin, seq, *, tm=256):
    t, d = x.shape
    ql, kl = qg.shape[1], kvg.shape[1]
    hn = wqn.shape[1]
    n_heads = hn // LANES
    tm = min(tm, seq)
    tps = seq // tm
    assert seq % tm == 0
    scale = float(QK_NOPE_DIM + QK_ROPE_DIM) ** -0.5
    row = lambda i: (i, 0)
    return pl.pallas_call(
        functools.partial(_latent_kernel, ql=ql, kl=kl, n_heads=n_heads, scale=scale),
        out_shape=(
            jax.ShapeDtypeStruct((t, n_heads * QK_PAD_DIM), BF16),
            jax.ShapeDtypeStruct((t, n_heads * QK_PAD_DIM), BF16),
            jax.ShapeDtypeStruct((t, hn), BF16),
        ),
        grid=(t // tm,),
        in_specs=[
            pl.BlockSpec((tm, d), row),
            _resident((1, d)),
            _resident(wlat.shape),
            _resident((1, ql)),
            _resident((1, kl)),
            _resident(wqn.shape),
            _resident(wqr.shape),
            _resident(wqrr.shape),
            _resident(wk.shape),
            _resident(wv.shape),
            pl.BlockSpec((tm, LANES), lambda i: (i % tps, 0)),
            pl.BlockSpec((tm, LANES), lambda i: (i % tps, 0)),
        ],
        out_specs=(
            pl.BlockSpec((tm, n_heads * QK_PAD_DIM), row),
            pl.BlockSpec((tm, n_heads * QK_PAD_DIM), row),
            pl.BlockSpec((tm, hn), row),
        ),
        compiler_params=_params("parallel"),
        name="latent",
    )(x, g, wlat, qg, kvg, wqn, wqr, wqrr, wk, wv, cos, sin)


def _attn_kernel(q_ref, k_ref, v_ref, o_ref):
    s = lax.dot_general(q_ref[...], k_ref[...], (((1,), (1,)), ((), ())), preferred_element_type=F32)
    m = jnp.max(s, axis=1, keepdims=True)
    p = jnp.exp(s - m)
    l = jnp.sum(p, axis=1, keepdims=True)
    o = _dot(p.astype(BF16), v_ref[...])
    o_ref[...] = (o / l).astype(BF16)


def _attention(q, k, v, batch, seq, *, tq=256):
    t = q.shape[0]
    n_heads = q.shape[1] // QK_PAD_DIM
    dv = v.shape[1] // n_heads
    tq = min(tq, seq)
    tps = seq // tq
    assert seq % tq == 0
    return pl.pallas_call(
        _attn_kernel,
        out_shape=jax.ShapeDtypeStruct((t, n_heads * dv), BF16),
        grid=(batch, n_heads, tps),
        in_specs=[
            pl.BlockSpec((tq, QK_PAD_DIM), lambda b, h, i: (b * tps + i, h)),
            pl.BlockSpec((seq, QK_PAD_DIM), lambda b, h, i: (b, h)),
            pl.BlockSpec((seq, dv), lambda b, h, i: (b, h)),
        ],
        out_specs=pl.BlockSpec((tq, dv), lambda b, h, i: (b * tps + i, h)),
        compiler_params=_params("parallel", "parallel", "arbitrary"),
        name="attention",
    )(q, k, v)


def _gate_kernel(x_ref, g_ref, w_ref, b_ref, o_ref, h_ref):
    @pl.when(pl.program_id(1) == 0)
    def _():
        h_ref[...] = (_rms(x_ref[...]) * g_ref[...]).astype(BF16)

    o_ref[...] = jax.nn.sigmoid(_dot(h_ref[...], w_ref[...]) + b_ref[...]).astype(BF16)


def _gates(x, g, w, b, *, tm=512, tn=1024):
    t, d = x.shape
    n = w.shape[1]
    tm, tn = min(tm, t), min(tn, n)
    assert t % tm == 0 and n % tn == 0
    return pl.pallas_call(
        _gate_kernel,
        out_shape=jax.ShapeDtypeStruct((t, n), BF16),
        grid=(t // tm, n // tn),
        in_specs=[
            pl.BlockSpec((tm, d), lambda i, j: (i, 0)),
            _resident((1, d)),
            pl.BlockSpec((d, tn), lambda i, j: (0, j)),
            pl.BlockSpec((1, tn), lambda i, j: (0, j)),
        ],
        out_specs=pl.BlockSpec((tm, tn), lambda i, j: (i, j)),
        scratch_shapes=[pltpu.VMEM((tm, d), BF16)],
        compiler_params=_params("parallel", "arbitrary"),
        name="gates",
    )(x, g, w, b)


def _merge_kernel(x_ref, mix_ref, att_ref, gate_ref, wf_ref, wo_ref, wout_ref, o_ref, *, d):
    ya = _dot(mix_ref[...], wf_ref[...])
    yb = _dot(att_ref[...], wo_ref[...])
    m = gate_ref[:, :d].astype(F32) * ya + gate_ref[:, d:].astype(F32) * yb
    o_ref[...] = x_ref[...] + _dot(m.astype(BF16), wout_ref[...])


def _merge(x, mixed, att, gates, wf, wo, wout, *, tm=256):
    t, d = x.shape
    tm = min(tm, t)
    assert t % tm == 0
    row = lambda i: (i, 0)
    return pl.pallas_call(
        functools.partial(_merge_kernel, d=d),
        out_shape=jax.ShapeDtypeStruct((t, d), F32),
        grid=(t // tm,),
        in_specs=[
            pl.BlockSpec((tm, d), row),
            pl.BlockSpec((tm, mixed.shape[1]), row),
            pl.BlockSpec((tm, att.shape[1]), row),
            pl.BlockSpec((tm, gates.shape[1]), row),
            _resident(wf.shape),
            _resident(wo.shape),
            _resident(wout.shape),
        ],
        out_specs=pl.BlockSpec((tm, d), row),
        compiler_params=_params("parallel"),
        name="merge",
    )(x, mixed, att, gates, wf, wo, wout)


def _dft_tables(n):
    idx = jnp.arange(n, dtype=jnp.int32)
    jk = (idx[:, None] * idx[None, :]) % n
    ang = jk.astype(F32) * (2.0 * math.pi / n)
    s = n ** -0.5
    return jnp.cos(ang) * s, jnp.sin(ang) * s


def _rope_tables(seq):
    half = QK_ROPE_DIM // 2
    inv_freq = 1.0 / (ROPE_THETA ** (jnp.arange(half, dtype=F32) / half))
    ang = jnp.arange(seq, dtype=F32)[:, None] * inv_freq[None, :]
    pad = jnp.zeros((seq, LANES - QK_ROPE_DIM), F32)
    cos = jnp.concatenate([jnp.cos(ang), jnp.cos(ang), pad], axis=1)
    sin = jnp.concatenate([jnp.sin(ang), jnp.sin(ang), pad], axis=1)
    return cos, sin


def _rot_cols(w):
    half = QK_ROPE_DIM // 2
    return jnp.concatenate([-w[..., half:], w[..., :half]], axis=-1)


def _pad_lanes(w):
    return jnp.concatenate([w, jnp.zeros(w.shape[:-1] + (LANES - w.shape[-1],), w.dtype)], axis=-1)


def _prep_layer(l, p):
    d = p["w_in"].shape[1]
    ql, kl = p["q_a_norm"].shape[1], p["kv_a_norm"].shape[1]
    fw = p["w_fourier"].shape[1]
    w_in = p["w_in"][l]
    s0, s1, s2, s3 = fw, fw + ql, fw + ql + kl, fw + ql + kl + QK_ROPE_DIM
    w_kr = w_in[:, s2:s3]
    wlat = jnp.concatenate([w_in[:, s0:s2], _pad_lanes(w_kr), _pad_lanes(_rot_cols(w_kr))], axis=1)
    w_uq = p["w_uq"][l].reshape(ql, N_HEADS, QK_NOPE_DIM + QK_ROPE_DIM)
    w_q_rope = w_uq[:, :, QK_NOPE_DIM:]
    w_ukv = p["w_ukv"][l].reshape(kl, N_HEADS, QK_NOPE_DIM + V_HEAD_DIM)
    bf = lambda a: a.astype(BF16)
    return dict(
        ffn1=(p["ffn1_norm"][l][None], bf(p["ffn1_w_gate"][l]), bf(p["ffn1_w_up"][l]), bf(p["ffn1_w_down"][l])),
        ffn2=(p["ffn2_norm"][l][None], bf(p["ffn2_w_gate"][l]), bf(p["ffn2_w_up"][l]), bf(p["ffn2_w_down"][l])),
        mix_norm=p["mix_norm"][l][None],
        wf_in=bf(w_in[:, :s0]),
        wlat=bf(wlat),
        w_gate=bf(w_in[:, s3:]),
        b_gate=p["b_gate"][l][None],
        qg=p["q_a_norm"][l][None],
        kvg=p["kv_a_norm"][l][None],
        wqn=bf(w_uq[:, :, :QK_NOPE_DIM].reshape(ql, N_HEADS * QK_NOPE_DIM)),
        wqr=bf(_pad_lanes(w_q_rope).reshape(ql, N_HEADS * LANES)),
        wqrr=bf(_pad_lanes(_rot_cols(w_q_rope)).reshape(ql, N_HEADS * LANES)),
        wk=bf(w_ukv[:, :, :QK_NOPE_DIM].reshape(kl, N_HEADS * QK_NOPE_DIM)),
        wv=bf(w_ukv[:, :, QK_NOPE_DIM:].reshape(kl, N_HEADS * V_HEAD_DIM)),
        w_fourier=bf(p["w_fourier"][l]),
        w_mla_o=bf(p["w_mla_o"][l]),
        w_out=bf(p["w_out"][l]),
    )


def _trunk(x3, layers, final_g, cs):
    batch, seq, d = x3.shape
    x = x3.reshape(batch * seq, d)
    cos_s, sin_s = _dft_tables(seq)
    fmat = jnp.concatenate([cos_s, -sin_s], axis=1).astype(BF16)
    rope_cos, rope_sin = _rope_tables(seq)
    for li, w in enumerate(layers):
        x = _ffn(x, *w["ffn1"])
        r = _fourier_in(x, w["mix_norm"], w["wf_in"], cs, batch, seq)
        q, k, v = _latent(x, w["mix_norm"], w["wlat"], w["qg"], w["kvg"], w["wqn"], w["wqr"], w["wqrr"],
                          w["wk"], w["wv"], rope_cos, rope_sin, seq)
        gates = _gates(x, w["mix_norm"], w["w_gate"], w["b_gate"])
        mixed = _seq_dft(fmat, r.reshape(batch, 2 * seq, r.shape[-1]))
        att = _attention(q, k, v, batch, seq)
        x = _merge(x, mixed, att, gates, w["w_fourier"], w["w_mla_o"], w["w_out"])
        x = _ffn(x, *w["ffn2"], final_g=final_g if li == len(layers) - 1 else None)
    return x.reshape(batch, seq, d)


def kernel(x_prompt, x_sample, ffn1_norm, ffn1_w_gate, ffn1_w_up, ffn1_w_down, mix_norm, w_in, b_gate, q_a_norm, kv_a_norm, w_uq, w_ukv, w_fourier, w_mla_o, w_out, ffn2_norm, ffn2_w_gate, ffn2_w_up, ffn2_w_down, final_norm):
    p = dict(ffn1_norm=ffn1_norm, ffn1_w_gate=ffn1_w_gate, ffn1_w_up=ffn1_w_up, ffn1_w_down=ffn1_w_down,
             mix_norm=mix_norm, w_in=w_in, b_gate=b_gate, q_a_norm=q_a_norm, kv_a_norm=kv_a_norm, w_uq=w_uq,
             w_ukv=w_ukv, w_fourier=w_fourier, w_mla_o=w_mla_o, w_out=w_out, ffn2_norm=ffn2_norm,
             ffn2_w_gate=ffn2_w_gate, ffn2_w_up=ffn2_w_up, ffn2_w_down=ffn2_w_down)
    layers = [_prep_layer(l, p) for l in range(w_in.shape[0])]
    gd = w_fourier.shape[1] // N_FOURIER_GROUPS
    cos_c, sin_c = _dft_tables(gd)
    cs = jnp.concatenate([cos_c, sin_c], axis=1).astype(BF16)
    final_g = final_norm[None]
    return (_trunk(x_prompt, layers, final_g, cs), _trunk(x_sample, layers, final_g, cs))
```

```python
import functools
import math

import jax
import jax.numpy as jnp
from jax import lax
from jax.experimental import pallas as pl
from jax.experimental.pallas import tpu as pltpu

N_HEADS = 16
QK_NOPE_DIM = 128
QK_ROPE_DIM = 64
V_HEAD_DIM = 128
N_FOURIER_GROUPS = 4
N_BRANCHES = 2
ROPE_THETA = 10000.0
NORM_EPS = 1e-6

LANES = 128
QK_PAD_DIM = 2 * LANES
VT_ROWS = V_HEAD_DIM + 16
VMEM_LIMIT_BYTES = 56 * 1024 * 1024

F32 = jnp.float32
BF16 = jnp.bfloat16
_NT = (((1,), (1,)), ((), ()))


def _params(*semantics):
    return pltpu.CompilerParams(dimension_semantics=semantics, vmem_limit_bytes=VMEM_LIMIT_BYTES)


def _resident(shape):
    zeros = (0,) * len(shape)
    return pl.BlockSpec(shape, lambda *_: zeros, pipeline_mode=pl.Buffered(1))


def _rms(x):
    return x * lax.rsqrt(jnp.mean(x * x, axis=-1, keepdims=True) + NORM_EPS)


def _dot(a, b):
    return jnp.dot(a, b, preferred_element_type=F32)


def _ffn_kernel(x_ref, g_ref, wg_ref, wu_ref, wd_ref, *rest, n_ff, final):
    if final:
        fg_ref, o_ref, h_ref = rest
    else:
        o_ref, h_ref = rest
    f = pl.program_id(1)

    @pl.when(f == 0)
    def _():
        x = x_ref[...]
        h_ref[...] = (_rms(x) * g_ref[...]).astype(BF16)
        o_ref[...] = x

    h = h_ref[...]
    a = _dot(h, wg_ref[...])
    u = _dot(h, wu_ref[...])
    act = (0.5 * (a * jax.nn.sigmoid(a)) * u).astype(BF16)
    o_ref[...] += _dot(act, wd_ref[...])

    if final:
        @pl.when(f == n_ff - 1)
        def _():
            o_ref[...] = _rms(o_ref[...]) * fg_ref[...]


def _ffn(x, g, wg, wu, wd, final_g=None, *, tm=512, tf=512):
    t, d = x.shape
    d_ff = wg.shape[1]
    tm, tf = min(tm, t), min(tf, d_ff)
    n_ff = d_ff // tf
    assert t % tm == 0 and d_ff % tf == 0
    final = final_g is not None
    in_specs = [
        pl.BlockSpec((tm, d), lambda i, f: (i, 0)),
        _resident((1, d)),
        pl.BlockSpec((d, tf), lambda i, f: (0, f)),
        pl.BlockSpec((d, tf), lambda i, f: (0, f)),
        pl.BlockSpec((tf, d), lambda i, f: (f, 0)),
    ]
    args = [x, g, wg, wu, wd]
    if final:
        in_specs.append(_resident((1, d)))
        args.append(final_g)
    return pl.pallas_call(
        functools.partial(_ffn_kernel, n_ff=n_ff, final=final),
        out_shape=jax.ShapeDtypeStruct((t, d), F32),
        grid=(t // tm, n_ff),
        in_specs=in_specs,
        out_specs=pl.BlockSpec((tm, d), lambda i, f: (i, 0)),
        scratch_shapes=[pltpu.VMEM((tm, d), BF16)],
        compiler_params=_params("parallel", "arbitrary"),
        name="ffn_final" if final else "ffn",
    )(*args)


def _fourier_in_kernel(x_ref, g_ref, wf_ref, cs_ref, o_ref, *, gd, n_groups):
    h = (_rms(x_ref[...]) * g_ref[...]).astype(BF16)
    u = _dot(h, wf_ref[...]).astype(BF16)
    cs = cs_ref[...]
    for grp in range(n_groups):
        cols = slice(grp * gd, (grp + 1) * gd)
        a = _dot(u[:, cols], cs)
        o_ref[0, :, cols] = a[:, :gd].astype(BF16)
        o_ref[1, :, cols] = a[:, gd:].astype(BF16)


def _fourier_in(x, g, wf, cs, batch, seq, *, tm=512):
    t, d = x.shape
    fw = wf.shape[1]
    gd = fw // N_FOURIER_GROUPS
    tm = min(tm, seq)
    tps = seq // tm
    assert seq % tm == 0
    return pl.pallas_call(
        functools.partial(_fourier_in_kernel, gd=gd, n_groups=N_FOURIER_GROUPS),
        out_shape=jax.ShapeDtypeStruct((batch, 2, seq, fw), BF16),
        grid=(t // tm,),
        in_specs=[
            pl.BlockSpec((tm, d), lambda i: (i, 0)),
            _resident((1, d)),
            _resident((d, fw)),
            _resident((gd, 2 * gd)),
        ],
        out_specs=pl.BlockSpec((None, 2, tm, fw), lambda i: (i // tps, 0, i % tps, 0)),
        compiler_params=_params("parallel"),
        name="fourier_in",
    )(x, g, wf, cs)


def _seq_dft_kernel(f_ref, r_ref, o_ref):
    o_ref[...] = _dot(f_ref[...], r_ref[...]).astype(BF16)


def _seq_dft(fmat, r, *, tm=512, tn=512):
    batch, s2, fw = r.shape
    seq = s2 // 2
    tm, tn = min(tm, seq), min(tn, fw)
    tps = seq // tm
    assert seq % tm == 0 and fw % tn == 0
    return pl.pallas_call(
        _seq_dft_kernel,
        out_shape=jax.ShapeDtypeStruct((batch * seq, fw), BF16),
        grid=(batch, fw // tn, tps),
        in_specs=[
            pl.BlockSpec((tm, s2), lambda b, n, i: (i, 0)),
            pl.BlockSpec((None, s2, tn), lambda b, n, i: (b, 0, n)),
        ],
        out_specs=pl.BlockSpec((tm, tn), lambda b, n, i: (b * tps + i, n)),
        compiler_params=_params("parallel", "parallel", "arbitrary"),
        name="seq_dft",
    )(fmat, r)


def _latent_kernel(x_ref, g_ref, wlat_ref, qg_ref, kvg_ref, wqn_ref, wqr_ref, wqrr_ref, wk_ref, wvt_ref,
                   cos_ref, sin_ref, q_ref, k_ref, vt_ref, *, ql, kl, n_heads, dv, scale):
    h = (_rms(x_ref[...]) * g_ref[...]).astype(BF16)
    lat = _dot(h, wlat_ref[...])
    cq = (_rms(lat[:, :ql]) * qg_ref[...]).astype(BF16)
    ckv = (_rms(lat[:, ql:ql + kl]) * kvg_ref[...]).astype(BF16)
    cos = cos_ref[...]
    sin = sin_ref[...]
    kr = lat[:, ql + kl:ql + kl + LANES]
    krr = lat[:, ql + kl + LANES:ql + kl + 2 * LANES]
    k_rope = (kr * cos + krr * sin).astype(BF16)
    qn = _dot(cq, wqn_ref[...]) * scale
    qa = _dot(cq, wqr_ref[...])
    qb = _dot(cq, wqrr_ref[...])
    kn = _dot(ckv, wk_ref[...]).astype(BF16)
    vt = lax.dot_general(wvt_ref[...], ckv, _NT, preferred_element_type=F32).astype(BF16)
    ones = jnp.ones((VT_ROWS - dv, vt.shape[1]), BF16)
    for hd in range(n_heads):
        src = slice(hd * LANES, (hd + 1) * LANES)
        lo = slice(hd * QK_PAD_DIM, hd * QK_PAD_DIM + LANES)
        hi = slice(hd * QK_PAD_DIM + LANES, (hd + 1) * QK_PAD_DIM)
        q_ref[:, lo] = qn[:, src].astype(BF16)
        q_ref[:, hi] = ((qa[:, src] * cos + qb[:, src] * sin) * scale).astype(BF16)
        k_ref[:, lo] = kn[:, src]
        k_ref[:, hi] = k_rope
        vt_ref[hd, :dv, :] = vt[hd * dv:(hd + 1) * dv, :]
        vt_ref[hd, dv:, :] = ones


def _latent(x, g, wlat, qg, kvg, wqn, wqr, wqrr, wk, wvt, cos, sin, batch, seq, *, tm=256):
    t, d = x.shape
    ql, kl = qg.shape[1], kvg.shape[1]
    hn = wqn.shape[1]
    n_heads = hn // LANES
    dv = wvt.shape[0] // n_heads
    tm = min(tm, seq)
    tps = seq // tm
    assert seq % tm == 0
    scale = float(QK_NOPE_DIM + QK_ROPE_DIM) ** -0.5 * math.log2(math.e)
    row = lambda i: (i, 0)
    return pl.pallas_call(
        functools.partial(_latent_kernel, ql=ql, kl=kl, n_heads=n_heads, dv=dv, scale=scale),
        out_shape=(
            jax.ShapeDtypeStruct((t, n_heads * QK_PAD_DIM), BF16),
            jax.ShapeDtypeStruct((t, n_heads * QK_PAD_DIM), BF16),
            jax.ShapeDtypeStruct((batch, n_heads, VT_ROWS, seq), BF16),
        ),
        grid=(t // tm,),
        in_specs=[
            pl.BlockSpec((tm, d), row),
            _resident((1, d)),
            _resident(wlat.shape),
            _resident((1, ql)),
            _resident((1, kl)),
            _resident(wqn.shape),
            _resident(wqr.shape),
            _resident(wqrr.shape),
            _resident(wk.shape),
            _resident(wvt.shape),
            pl.BlockSpec((tm, LANES), lambda i: (i % tps, 0)),
            pl.BlockSpec((tm, LANES), lambda i: (i % tps, 0)),
        ],
        out_specs=(
            pl.BlockSpec((tm, n_heads * QK_PAD_DIM), row),
            pl.BlockSpec((tm, n_heads * QK_PAD_DIM), row),
            pl.BlockSpec((None, n_heads, VT_ROWS, tm), lambda i: (i // tps, 0, 0, i % tps)),
        ),
        compiler_params=_params("parallel"),
        name="latent",
    )(x, g, wlat, qg, kvg, wqn, wqr, wqrr, wk, wvt, cos, sin)


def _attn_kernel(q_ref, k_ref, vt_ref, o_ref, st_ref, pt_ref, *, n_sub, tsub, kc, dv):
    n_chunks = k_ref.shape[0] // kc
    col_max = [None] * n_sub
    for t in range(-1, n_sub + 1):
        run_max = None
        ot = None
        for c in range(n_chunks):
            rows = slice(c * kc, (c + 1) * kc)
            if t + 1 < n_sub:
                q = q_ref[(t + 1) * tsub:(t + 2) * tsub, :]
                s = lax.dot_general(k_ref[rows, :], q, _NT, preferred_element_type=F32)
                st_ref[(t + 1) % 2, rows, :] = s
                chunk_max = jnp.max(s, axis=0, keepdims=True)
                run_max = chunk_max if run_max is None else jnp.maximum(run_max, chunk_max)
            if 0 <= t < n_sub:
                pt_ref[t % 2, rows, :] = jnp.exp2(st_ref[t % 2, rows, :] - col_max[t]).astype(BF16)
            if t >= 1:
                part = _dot(vt_ref[:, rows], pt_ref[(t - 1) % 2, rows, :])
                ot = part if ot is None else ot + part
        if t + 1 < n_sub:
            col_max[t + 1] = run_max
        if t >= 1:
            o = ot[:dv] / ot[dv:dv + 1]
            o_ref[(t - 1) * tsub:t * tsub, :] = o.T.astype(BF16)


def _attention(q, k, vt, batch, seq, *, tq=2048, tsub=512, kc=512):
    t = q.shape[0]
    n_heads = q.shape[1] // QK_PAD_DIM
    dv = V_HEAD_DIM
    tq = min(tq, seq)
    tsub, kc = min(tsub, tq), min(kc, seq)
    tps = seq // tq
    assert seq % tq == 0 and tq % tsub == 0 and seq % kc == 0
    return pl.pallas_call(
        functools.partial(_attn_kernel, n_sub=tq // tsub, tsub=tsub, kc=kc, dv=dv),
        out_shape=jax.ShapeDtypeStruct((t, n_heads * dv), BF16),
        grid=(batch, n_heads, tps),
        in_specs=[
            pl.BlockSpec((tq, QK_PAD_DIM), lambda b, h, i: (b * tps + i, h)),
            pl.BlockSpec((seq, QK_PAD_DIM), lambda b, h, i: (b, h)),
            pl.BlockSpec((None, None, VT_ROWS, seq), lambda b, h, i: (b, h, 0, 0)),
        ],
        out_specs=pl.BlockSpec((tq, dv), lambda b, h, i: (b * tps + i, h)),
        scratch_shapes=[pltpu.VMEM((2, seq, tsub), F32), pltpu.VMEM((2, seq, tsub), BF16)],
        compiler_params=_params("parallel", "parallel", "arbitrary"),
        name="attention",
    )(q, k, vt)


def _gate_kernel(x_ref, g_ref, w_ref, b_ref, o_ref):
    h = (_rms(x_ref[...]) * g_ref[...]).astype(BF16)
    o_ref[...] = jax.nn.sigmoid(_dot(h, w_ref[...]) + b_ref[...]).astype(BF16)


def _gates(x, g, w, b, *, tm=512):
    t, d = x.shape
    n = w.shape[1]
    tm = min(tm, t)
    assert t % tm == 0
    return pl.pallas_call(
        _gate_kernel,
        out_shape=jax.ShapeDtypeStruct((t, n), BF16),
        grid=(t // tm,),
        in_specs=[
            pl.BlockSpec((tm, d), lambda i: (i, 0)),
            _resident((1, d)),
            _resident((d, n)),
            _resident((1, n)),
        ],
        out_specs=pl.BlockSpec((tm, n), lambda i: (i, 0)),
        compiler_params=_params("parallel"),
        name="gates",
    )(x, g, w, b)


def _merge_kernel(x_ref, mix_ref, att_ref, gate_ref, wf_ref, wo_ref, wout_ref, o_ref, *, d):
    ya = _dot(mix_ref[...], wf_ref[...])
    yb = _dot(att_ref[...], wo_ref[...])
    m = gate_ref[:, :d].astype(F32) * ya + gate_ref[:, d:].astype(F32) * yb
    o_ref[...] = x_ref[...] + _dot(m.astype(BF16), wout_ref[...])


def _merge(x, mixed, att, gates, wf, wo, wout, *, tm=256):
    t, d = x.shape
    tm = min(tm, t)
    assert t % tm == 0
    row = lambda i: (i, 0)
    return pl.pallas_call(
        functools.partial(_merge_kernel, d=d),
        out_shape=jax.ShapeDtypeStruct((t, d), F32),
        grid=(t // tm,),
        in_specs=[
            pl.BlockSpec((tm, d), row),
            pl.BlockSpec((tm, mixed.shape[1]), row),
            pl.BlockSpec((tm, att.shape[1]), row),
            pl.BlockSpec((tm, gates.shape[1]), row),
            _resident(wf.shape),
            _resident(wo.shape),
            _resident(wout.shape),
        ],
        out_specs=pl.BlockSpec((tm, d), row),
        compiler_params=_params("parallel"),
        name="merge",
    )(x, mixed, att, gates, wf, wo, wout)


def _dft_tables(n):
    idx = jnp.arange(n, dtype=jnp.int32)
    jk = (idx[:, None] * idx[None, :]) % n
    ang = jk.astype(F32) * (2.0 * math.pi / n)
    s = n ** -0.5
    return jnp.cos(ang) * s, jnp.sin(ang) * s


def _rope_tables(seq):
    half = QK_ROPE_DIM // 2
    inv_freq = 1.0 / (ROPE_THETA ** (jnp.arange(half, dtype=F32) / half))
    ang = jnp.arange(seq, dtype=F32)[:, None] * inv_freq[None, :]
    pad = jnp.zeros((seq, LANES - QK_ROPE_DIM), F32)
    cos = jnp.concatenate([jnp.cos(ang), jnp.cos(ang), pad], axis=1)
    sin = jnp.concatenate([jnp.sin(ang), jnp.sin(ang), pad], axis=1)
    return cos, sin


def _rot_cols(w):
    half = QK_ROPE_DIM // 2
    return jnp.concatenate([-w[..., half:], w[..., :half]], axis=-1)


def _pad_lanes(w):
    return jnp.concatenate([w, jnp.zeros(w.shape[:-1] + (LANES - w.shape[-1],), w.dtype)], axis=-1)


def _prep_layer(l, p):
    ql, kl = p["q_a_norm"].shape[1], p["kv_a_norm"].shape[1]
    fw = p["w_fourier"].shape[1]
    w_in = p["w_in"][l]
    s0, s1, s2, s3 = fw, fw + ql, fw + ql + kl, fw + ql + kl + QK_ROPE_DIM
    w_kr = w_in[:, s2:s3]
    wlat = jnp.concatenate([w_in[:, s0:s2], _pad_lanes(w_kr), _pad_lanes(_rot_cols(w_kr))], axis=1)
    w_uq = p["w_uq"][l].reshape(ql, N_HEADS, QK_NOPE_DIM + QK_ROPE_DIM)
    w_q_rope = w_uq[:, :, QK_NOPE_DIM:]
    w_ukv = p["w_ukv"][l].reshape(kl, N_HEADS, QK_NOPE_DIM + V_HEAD_DIM)
    bf = lambda a: a.astype(BF16)
    return dict(
        ffn1=(p["ffn1_norm"][l][None], bf(p["ffn1_w_gate"][l]), bf(p["ffn1_w_up"][l]), bf(p["ffn1_w_down"][l])),
        ffn2=(p["ffn2_norm"][l][None], bf(p["ffn2_w_gate"][l]), bf(p["ffn2_w_up"][l]), bf(p["ffn2_w_down"][l])),
        mix_norm=p["mix_norm"][l][None],
        wf_in=bf(w_in[:, :s0]),
        wlat=bf(wlat),
        w_gate=bf(w_in[:, s3:]),
        b_gate=p["b_gate"][l][None],
        qg=p["q_a_norm"][l][None],
        kvg=p["kv_a_norm"][l][None],
        wqn=bf(w_uq[:, :, :QK_NOPE_DIM].reshape(ql, N_HEADS * QK_NOPE_DIM)),
        wqr=bf(_pad_lanes(w_q_rope).reshape(ql, N_HEADS * LANES)),
        wqrr=bf(_pad_lanes(_rot_cols(w_q_rope)).reshape(ql, N_HEADS * LANES)),
        wk=bf(w_ukv[:, :, :QK_NOPE_DIM].reshape(kl, N_HEADS * QK_NOPE_DIM)),
        wvt=bf(w_ukv[:, :, QK_NOPE_DIM:].reshape(kl, N_HEADS * V_HEAD_DIM).T),
        w_fourier=bf(p["w_fourier"][l]),
        w_mla_o=bf(p["w_mla_o"][l]),
        w_out=bf(p["w_out"][l]),
    )


def _trunk(x3, layers, final_g, cs):
    batch, seq, d = x3.shape
    x = x3.reshape(batch * seq, d)
    cos_s, sin_s = _dft_tables(seq)
    fmat = jnp.concatenate([cos_s, -sin_s], axis=1).astype(BF16)
    rope_cos, rope_sin = _rope_tables(seq)
    for li, w in enumerate(layers):
        x = _ffn(x, *w["ffn1"])
        r = _fourier_in(x, w["mix_norm"], w["wf_in"], cs, batch, seq)
        q, k, vt = _latent(x, w["mix_norm"], w["wlat"], w["qg"], w["kvg"], w["wqn"], w["wqr"], w["wqrr"],
                           w["wk"], w["wvt"], rope_cos, rope_sin, batch, seq)
        gates = _gates(x, w["mix_norm"], w["w_gate"], w["b_gate"])
        mixed = _seq_dft(fmat, r.reshape(batch, 2 * seq, r.shape[-1]))
        att = _attention(q, k, vt, batch, seq)
        x = _merge(x, mixed, att, gates, w["w_fourier"], w["w_mla_o"], w["w_out"])
        x = _ffn(x, *w["ffn2"], final_g=final_g if li == len(layers) - 1 else None)
    return x.reshape(batch, seq, d)


def kernel(x_prompt, x_sample, ffn1_norm, ffn1_w_gate, ffn1_w_up, ffn1_w_down, mix_norm, w_in, b_gate, q_a_norm, kv_a_norm, w_uq, w_ukv, w_fourier, w_mla_o, w_out, ffn2_norm, ffn2_w_gate, ffn2_w_up, ffn2_w_down, final_norm):
    p = dict(ffn1_norm=ffn1_norm, ffn1_w_gate=ffn1_w_gate, ffn1_w_up=ffn1_w_up, ffn1_w_down=ffn1_w_down,
             mix_norm=mix_norm, w_in=w_in, b_gate=b_gate, q_a_norm=q_a_norm, kv_a_norm=kv_a_norm, w_uq=w_uq,
             w_ukv=w_ukv, w_fourier=w_fourier, w_mla_o=w_mla_o, w_out=w_out, ffn2_norm=ffn2_norm,
             ffn2_w_gate=ffn2_w_gate, ffn2_w_up=ffn2_w_up, ffn2_w_down=ffn2_w_down)
    layers = [_prep_layer(l, p) for l in range(w_in.shape[0])]
    gd = w_fourier.shape[1] // N_FOURIER_GROUPS
    cos_c, sin_c = _dft_tables(gd)
    cs = jnp.concatenate([cos_c, sin_c], axis=1).astype(BF16)
    final_g = final_norm[None]
    return (_trunk(x_prompt, layers, final_g, cs), _trunk(x_sample, layers, final_g, cs))
```

```python
import functools
import math

import jax
import jax.numpy as jnp
from jax import lax
from jax.experimental import pallas as pl
from jax.experimental.pallas import tpu as pltpu

N_HEADS = 16
QK_NOPE_DIM = 128
QK_ROPE_DIM = 64
V_HEAD_DIM = 128
N_FOURIER_GROUPS = 4
N_BRANCHES = 2
ROPE_THETA = 10000.0
NORM_EPS = 1e-6

LANES = 128
QK_PAD_DIM = 2 * LANES
VT_ROWS = V_HEAD_DIM + 16
ATTN_KEY_CHUNKS = 4
FFN_TILE = 512
DFT_ROW_SPLIT = 64
VMEM_LIMIT_BYTES = 56 * 1024 * 1024

F32 = jnp.float32
BF16 = jnp.bfloat16
_NT = (((1,), (1,)), ((), ()))


def _params(*semantics):
    return pltpu.CompilerParams(dimension_semantics=semantics, vmem_limit_bytes=VMEM_LIMIT_BYTES)


def _resident(shape):
    zeros = (0,) * len(shape)
    return pl.BlockSpec(shape, lambda *_: zeros, pipeline_mode=pl.Buffered(1))


def _rms(x):
    return x * lax.rsqrt(jnp.mean(x * x, axis=-1, keepdims=True) + NORM_EPS)


def _dot(a, b):
    return jnp.dot(a, b, preferred_element_type=F32)


def _ffn_kernel(x_ref, g_ref, wgu_ref, wd_ref, *rest, n_ff, tf, final):
    if final:
        fg_ref, o_ref, h_ref = rest
    else:
        o_ref, h_ref = rest
    f = pl.program_id(1)

    @pl.when(f == 0)
    def _():
        x = x_ref[...]
        h_ref[...] = (_rms(x) * g_ref[...]).astype(BF16)
        o_ref[...] = x

    au = _dot(h_ref[...], wgu_ref[...])
    a, u = au[:, :tf], au[:, tf:]
    act = (0.5 * (a * jax.nn.sigmoid(a)) * u).astype(BF16)
    o_ref[...] += _dot(act, wd_ref[...])

    if final:
        @pl.when(f == n_ff - 1)
        def _():
            o_ref[...] = _rms(o_ref[...]) * fg_ref[...]


def _ffn(x, g, wgu, wd, final_g=None, *, tm=512):
    t, d = x.shape
    n_ff, _, tf2 = wgu.shape
    tf = tf2 // 2
    tm = min(tm, t)
    assert t % tm == 0 and wd.shape[0] == n_ff * tf
    final = final_g is not None
    in_specs = [
        pl.BlockSpec((tm, d), lambda i, f: (i, 0)),
        _resident((1, d)),
        pl.BlockSpec((None, d, tf2), lambda i, f: (f, 0, 0)),
        pl.BlockSpec((tf, d), lambda i, f: (f, 0)),
    ]
    args = [x, g, wgu, wd]
    if final:
        in_specs.append(_resident((1, d)))
        args.append(final_g)
    return pl.pallas_call(
        functools.partial(_ffn_kernel, n_ff=n_ff, tf=tf, final=final),
        out_shape=jax.ShapeDtypeStruct((t, d), F32),
        grid=(t // tm, n_ff),
        in_specs=in_specs,
        out_specs=pl.BlockSpec((tm, d), lambda i, f: (i, 0)),
        scratch_shapes=[pltpu.VMEM((tm, d), BF16)],
        compiler_params=_params("parallel", "arbitrary"),
        name="ffn_final" if final else "ffn",
    )(*args)


def _fourier_in_kernel(x_ref, g_ref, wf_ref, cs_ref, o_ref, *, gd, n_groups):
    h = (_rms(x_ref[...]) * g_ref[...]).astype(BF16)
    u = _dot(h, wf_ref[...]).astype(BF16)
    cs = cs_ref[...]
    for grp in range(n_groups):
        cols = slice(grp * gd, (grp + 1) * gd)
        a = _dot(u[:, cols], cs)
        o_ref[0, :, cols] = a[:, :gd].astype(BF16)
        o_ref[1, :, cols] = a[:, gd:].astype(BF16)


def _fourier_in(x, g, wf, cs, batch, seq, *, tm=512):
    t, d = x.shape
    fw = wf.shape[1]
    gd = fw // N_FOURIER_GROUPS
    tm = min(tm, seq)
    tps = seq // tm
    assert seq % tm == 0
    return pl.pallas_call(
        functools.partial(_fourier_in_kernel, gd=gd, n_groups=N_FOURIER_GROUPS),
        out_shape=jax.ShapeDtypeStruct((batch, 2, seq, fw), BF16),
        grid=(t // tm,),
        in_specs=[
            pl.BlockSpec((tm, d), lambda i: (i, 0)),
            _resident((1, d)),
            _resident((d, fw)),
            _resident((gd, 2 * gd)),
        ],
        out_specs=pl.BlockSpec((None, 2, tm, fw), lambda i: (i // tps, 0, i % tps, 0)),
        compiler_params=_params("parallel"),
        name="fourier_in",
    )(x, g, wf, cs)


def _seq_dft_kernel(f_ref, r_ref, o_ref):
    o_ref[...] = _dot(f_ref[...], r_ref[...]).astype(BF16)


def _seq_dft(fmat, r, *, tm=512, tn=512):
    batch, s2, fw = r.shape
    seq = s2 // 2
    tm, tn = min(tm, seq), min(tn, fw)
    tps = seq // tm
    assert seq % tm == 0 and fw % tn == 0
    return pl.pallas_call(
        _seq_dft_kernel,
        out_shape=jax.ShapeDtypeStruct((batch * seq, fw), BF16),
        grid=(batch, fw // tn, tps),
        in_specs=[
            pl.BlockSpec((tm, s2), lambda b, n, i: (i, 0)),
            pl.BlockSpec((None, s2, tn), lambda b, n, i: (b, 0, n)),
        ],
        out_specs=pl.BlockSpec((tm, tn), lambda b, n, i: (b * tps + i, n)),
        compiler_params=_params("parallel", "parallel", "arbitrary"),
        name="seq_dft",
    )(fmat, r)


def _latent_kernel(x_ref, g_ref, wlat_ref, qg_ref, kvg_ref, wqn_ref, wqr_ref, wqrr_ref, wk_ref, wvt_ref,
                   cos_ref, sin_ref, q_ref, k_ref, vt_ref, *, ql, kl, n_heads, dv, scale):
    h = (_rms(x_ref[...]) * g_ref[...]).astype(BF16)
    lat = _dot(h, wlat_ref[...])
    cq = (_rms(lat[:, :ql]) * qg_ref[...]).astype(BF16)
    ckv = (_rms(lat[:, ql:ql + kl]) * kvg_ref[...]).astype(BF16)
    cos = cos_ref[...]
    sin = sin_ref[...]
    kr = lat[:, ql + kl:ql + kl + LANES]
    krr = lat[:, ql + kl + LANES:ql + kl + 2 * LANES]
    k_rope = (kr * cos + krr * sin).astype(BF16)
    qn = _dot(cq, wqn_ref[...]) * scale
    qa = _dot(cq, wqr_ref[...])
    qb = _dot(cq, wqrr_ref[...])
    kn = _dot(ckv, wk_ref[...]).astype(BF16)
    vt = lax.dot_general(wvt_ref[...], ckv, _NT, preferred_element_type=F32).astype(BF16)
    ones = jnp.ones((VT_ROWS - dv, vt.shape[1]), BF16)
    for hd in range(n_heads):
        src = slice(hd * LANES, (hd + 1) * LANES)
        lo = slice(hd * QK_PAD_DIM, hd * QK_PAD_DIM + LANES)
        hi = slice(hd * QK_PAD_DIM + LANES, (hd + 1) * QK_PAD_DIM)
        q_ref[:, lo] = qn[:, src].astype(BF16)
        q_ref[:, hi] = ((qa[:, src] * cos + qb[:, src] * sin) * scale).astype(BF16)
        k_ref[:, lo] = kn[:, src]
        k_ref[:, hi] = k_rope
        vt_ref[hd, :dv, :] = vt[hd * dv:(hd + 1) * dv, :]
        vt_ref[hd, dv:, :] = ones


def _latent(x, g, wlat, qg, kvg, wqn, wqr, wqrr, wk, wvt, cos, sin, batch, seq, *, tm=256):
    t, d = x.shape
    ql, kl = qg.shape[1], kvg.shape[1]
    hn = wqn.shape[1]
    n_heads = hn // LANES
    dv = wvt.shape[0] // n_heads
    tm = min(tm, seq)
    tps = seq // tm
    assert seq % tm == 0
    scale = float(QK_NOPE_DIM + QK_ROPE_DIM) ** -0.5 * math.log2(math.e)
    row = lambda i: (i, 0)
    return pl.pallas_call(
        functools.partial(_latent_kernel, ql=ql, kl=kl, n_heads=n_heads, dv=dv, scale=scale),
        out_shape=(
            jax.ShapeDtypeStruct((t, n_heads * QK_PAD_DIM), BF16),
            jax.ShapeDtypeStruct((t, n_heads * QK_PAD_DIM), BF16),
            jax.ShapeDtypeStruct((batch, n_heads, VT_ROWS, seq), BF16),
        ),
        grid=(t // tm,),
        in_specs=[
            pl.BlockSpec((tm, d), row),
            _resident((1, d)),
            _resident(wlat.shape),
            _resident((1, ql)),
            _resident((1, kl)),
            _resident(wqn.shape),
            _resident(wqr.shape),
            _resident(wqrr.shape),
            _resident(wk.shape),
            _resident(wvt.shape),
            pl.BlockSpec((tm, LANES), lambda i: (i % tps, 0)),
            pl.BlockSpec((tm, LANES), lambda i: (i % tps, 0)),
        ],
        out_specs=(
            pl.BlockSpec((tm, n_heads * QK_PAD_DIM), row),
            pl.BlockSpec((tm, n_heads * QK_PAD_DIM), row),
            pl.BlockSpec((None, n_heads, VT_ROWS, tm), lambda i: (i // tps, 0, 0, i % tps)),
        ),
        compiler_params=_params("parallel"),
        name="latent",
    )(x, g, wlat, qg, kvg, wqn, wqr, wqrr, wk, wvt, cos, sin)


def _attn_kernel(q_ref, k_ref, vt_ref, o_ref, st_ref, pt_ref, *, n_sub, tsub, kc, dv):
    n_chunks = k_ref.shape[0] // kc
    col_max = [None] * n_sub
    for t in range(-1, n_sub + 1):
        run_max = None
        ot = None
        for c in range(n_chunks):
            rows = slice(c * kc, (c + 1) * kc)
            if t + 1 < n_sub:
                q = q_ref[(t + 1) * tsub:(t + 2) * tsub, :]
                s = lax.dot_general(k_ref[rows, :], q, _NT, preferred_element_type=F32)
                st_ref[(t + 1) % 2, rows, :] = s
                chunk_max = jnp.max(s, axis=0, keepdims=True)
                run_max = chunk_max if run_max is None else jnp.maximum(run_max, chunk_max)
            if 0 <= t < n_sub:
                pt_ref[t % 2, rows, :] = jnp.exp2(st_ref[t % 2, rows, :] - col_max[t]).astype(BF16)
            if t >= 1:
                part = _dot(vt_ref[:, rows], pt_ref[(t - 1) % 2, rows, :])
                ot = part if ot is None else ot + part
        if t + 1 < n_sub:
            col_max[t + 1] = run_max
        if t >= 1:
            o = ot[:dv] / ot[dv:dv + 1]
            o_ref[(t - 1) * tsub:t * tsub, :] = o.T.astype(BF16)


def _attention(q, k, vt, batch, seq, *, tq=2048, tsub=512, kc=None):
    t = q.shape[0]
    n_heads = q.shape[1] // QK_PAD_DIM
    dv = V_HEAD_DIM
    tq = min(tq, seq)
    tsub = min(tsub, tq)
    kc = seq // ATTN_KEY_CHUNKS if kc is None else min(kc, seq)
    tps = seq // tq
    assert seq % tq == 0 and tq % tsub == 0 and seq % kc == 0
    return pl.pallas_call(
        functools.partial(_attn_kernel, n_sub=tq // tsub, tsub=tsub, kc=kc, dv=dv),
        out_shape=jax.ShapeDtypeStruct((t, n_heads * dv), BF16),
        grid=(batch, n_heads, tps),
        in_specs=[
            pl.BlockSpec((tq, QK_PAD_DIM), lambda b, h, i: (b * tps + i, h)),
            pl.BlockSpec((seq, QK_PAD_DIM), lambda b, h, i: (b, h)),
            pl.BlockSpec((None, None, VT_ROWS, seq), lambda b, h, i: (b, h, 0, 0)),
        ],
        out_specs=pl.BlockSpec((tq, dv), lambda b, h, i: (b * tps + i, h)),
        scratch_shapes=[pltpu.VMEM((2, seq, tsub), F32), pltpu.VMEM((2, seq, tsub), BF16)],
        compiler_params=_params("parallel", "parallel", "arbitrary"),
        name="attention",
    )(q, k, vt)


def _gate_kernel(x_ref, g_ref, w_ref, b_ref, o_ref):
    h = (_rms(x_ref[...]) * g_ref[...]).astype(BF16)
    o_ref[...] = jax.nn.sigmoid(_dot(h, w_ref[...]) + b_ref[...]).astype(BF16)


def _gates(x, g, w, b, *, tm=512):
    t, d = x.shape
    n = w.shape[1]
    tm = min(tm, t)
    assert t % tm == 0
    return pl.pallas_call(
        _gate_kernel,
        out_shape=jax.ShapeDtypeStruct((t, n), BF16),
        grid=(t // tm,),
        in_specs=[
            pl.BlockSpec((tm, d), lambda i: (i, 0)),
            _resident((1, d)),
            _resident((d, n)),
            _resident((1, n)),
        ],
        out_specs=pl.BlockSpec((tm, n), lambda i: (i, 0)),
        compiler_params=_params("parallel"),
        name="gates",
    )(x, g, w, b)


def _merge_kernel(x_ref, mix_ref, att_ref, gate_ref, wf_ref, wo_ref, wout_ref, o_ref, *, d):
    ya = _dot(mix_ref[...], wf_ref[...])
    yb = _dot(att_ref[...], wo_ref[...])
    m = gate_ref[:, :d].astype(F32) * ya + gate_ref[:, d:].astype(F32) * yb
    o_ref[...] = x_ref[...] + _dot(m.astype(BF16), wout_ref[...])


def _merge(x, mixed, att, gates, wf, wo, wout, *, tm=256):
    t, d = x.shape
    tm = min(tm, t)
    assert t % tm == 0
    row = lambda i: (i, 0)
    return pl.pallas_call(
        functools.partial(_merge_kernel, d=d),
        out_shape=jax.ShapeDtypeStruct((t, d), F32),
        grid=(t // tm,),
        in_specs=[
            pl.BlockSpec((tm, d), row),
            pl.BlockSpec((tm, mixed.shape[1]), row),
            pl.BlockSpec((tm, att.shape[1]), row),
            pl.BlockSpec((tm, gates.shape[1]), row),
            _resident(wf.shape),
            _resident(wo.shape),
            _resident(wout.shape),
        ],
        out_specs=pl.BlockSpec((tm, d), row),
        compiler_params=_params("parallel"),
        name="merge",
    )(x, mixed, att, gates, wf, wo, wout)


def _dft_tables(n):
    scale = n ** -0.5

    def trig(rows, k):
        ang = ((rows[:, None] * k[None, :]) % n).astype(F32) * (2.0 * math.pi / n)
        return jnp.cos(ang), jnp.sin(ang)

    k = jnp.arange(n, dtype=jnp.int32)
    m = DFT_ROW_SPLIT
    if n % m or n <= m:
        c, s = trig(k, k)
        return c * scale, s * scale
    ca, sa = trig(jnp.arange(n // m, dtype=jnp.int32) * m, k)
    cb, sb = trig(jnp.arange(m, dtype=jnp.int32), k)
    cb, sb = cb * scale, sb * scale
    c = ca[:, None, :] * cb[None, :, :] - sa[:, None, :] * sb[None, :, :]
    s = sa[:, None, :] * cb[None, :, :] + ca[:, None, :] * sb[None, :, :]
    return c.reshape(n, n), s.reshape(n, n)


def _rope_tables(seq):
    half = QK_ROPE_DIM // 2
    inv_freq = 1.0 / (ROPE_THETA ** (jnp.arange(half, dtype=F32) / half))
    ang = jnp.arange(seq, dtype=F32)[:, None] * inv_freq[None, :]
    pad = jnp.zeros((seq, LANES - QK_ROPE_DIM), F32)
    cos = jnp.concatenate([jnp.cos(ang), jnp.cos(ang), pad], axis=1)
    sin = jnp.concatenate([jnp.sin(ang), jnp.sin(ang), pad], axis=1)
    return cos, sin


def _rot_cols(w):
    half = QK_ROPE_DIM // 2
    return jnp.concatenate([-w[..., half:], w[..., :half]], axis=-1)


def _pad_lanes(w):
    return jnp.concatenate([w, jnp.zeros(w.shape[:-1] + (LANES - w.shape[-1],), w.dtype)], axis=-1)


def _ffn_gate_up(wg, wu):
    d, d_ff = wg.shape
    tf = min(FFN_TILE, d_ff)
    assert d_ff % tf == 0
    tiles = lambda w: w.astype(BF16).reshape(d, d_ff // tf, tf)
    return jnp.concatenate([tiles(wg), tiles(wu)], axis=2).transpose(1, 0, 2)


def _prep_layer(l, p):
    ql, kl = p["q_a_norm"].shape[1], p["kv_a_norm"].shape[1]
    fw = p["w_fourier"].shape[1]
    w_in = p["w_in"][l]
    s0, s1, s2, s3 = fw, fw + ql, fw + ql + kl, fw + ql + kl + QK_ROPE_DIM
    w_kr = w_in[:, s2:s3]
    wlat = jnp.concatenate([w_in[:, s0:s2], _pad_lanes(w_kr), _pad_lanes(_rot_cols(w_kr))], axis=1)
    w_uq = p["w_uq"][l].reshape(ql, N_HEADS, QK_NOPE_DIM + QK_ROPE_DIM)
    w_q_rope = w_uq[:, :, QK_NOPE_DIM:]
    w_ukv = p["w_ukv"][l].reshape(kl, N_HEADS, QK_NOPE_DIM + V_HEAD_DIM)
    bf = lambda a: a.astype(BF16)
    return dict(
        ffn1=(p["ffn1_norm"][l][None], _ffn_gate_up(p["ffn1_w_gate"][l], p["ffn1_w_up"][l]), bf(p["ffn1_w_down"][l])),
        ffn2=(p["ffn2_norm"][l][None], _ffn_gate_up(p["ffn2_w_gate"][l], p["ffn2_w_up"][l]), bf(p["ffn2_w_down"][l])),
        mix_norm=p["mix_norm"][l][None],
        wf_in=bf(w_in[:, :s0]),
        wlat=bf(wlat),
        w_gate=bf(w_in[:, s3:]),
        b_gate=p["b_gate"][l][None],
        qg=p["q_a_norm"][l][None],
        kvg=p["kv_a_norm"][l][None],
        wqn=bf(w_uq[:, :, :QK_NOPE_DIM].reshape(ql, N_HEADS * QK_NOPE_DIM)),
        wqr=bf(_pad_lanes(w_q_rope).reshape(ql, N_HEADS * LANES)),
        wqrr=bf(_pad_lanes(_rot_cols(w_q_rope)).reshape(ql, N_HEADS * LANES)),
        wk=bf(w_ukv[:, :, :QK_NOPE_DIM].reshape(kl, N_HEADS * QK_NOPE_DIM)),
        wvt=bf(w_ukv[:, :, QK_NOPE_DIM:].reshape(kl, N_HEADS * V_HEAD_DIM).T),
        w_fourier=bf(p["w_fourier"][l]),
        w_mla_o=bf(p["w_mla_o"][l]),
        w_out=bf(p["w_out"][l]),
    )


def _trunk(x3, layers, final_g, cs):
    batch, seq, d = x3.shape
    x = x3.reshape(batch * seq, d)
    cos_s, sin_s = _dft_tables(seq)
    fmat = jnp.concatenate([cos_s, -sin_s], axis=1).astype(BF16)
    rope_cos, rope_sin = _rope_tables(seq)
    for li, w in enumerate(layers):
        x = _ffn(x, *w["ffn1"])
        r = _fourier_in(x, w["mix_norm"], w["wf_in"], cs, batch, seq)
        q, k, vt = _latent(x, w["mix_norm"], w["wlat"], w["qg"], w["kvg"], w["wqn"], w["wqr"], w["wqrr"],
                           w["wk"], w["wvt"], rope_cos, rope_sin, batch, seq)
        gates = _gates(x, w["mix_norm"], w["w_gate"], w["b_gate"])
        mixed = _seq_dft(fmat, r.reshape(batch, 2 * seq, r.shape[-1]))
        att = _attention(q, k, vt, batch, seq)
        x = _merge(x, mixed, att, gates, w["w_fourier"], w["w_mla_o"], w["w_out"])
        x = _ffn(x, *w["ffn2"], final_g=final_g if li == len(layers) - 1 else None)
    return x.reshape(batch, seq, d)


def kernel(x_prompt, x_sample, ffn1_norm, ffn1_w_gate, ffn1_w_up, ffn1_w_down, mix_norm, w_in, b_gate, q_a_norm, kv_a_norm, w_uq, w_ukv, w_fourier, w_mla_o, w_out, ffn2_norm, ffn2_w_gate, ffn2_w_up, ffn2_w_down, final_norm):
    p = dict(ffn1_norm=ffn1_norm, ffn1_w_gate=ffn1_w_gate, ffn1_w_up=ffn1_w_up, ffn1_w_down=ffn1_w_down,
             mix_norm=mix_norm, w_in=w_in, b_gate=b_gate, q_a_norm=q_a_norm, kv_a_norm=kv_a_norm, w_uq=w_uq,
             w_ukv=w_ukv, w_fourier=w_fourier, w_mla_o=w_mla_o, w_out=w_out, ffn2_norm=ffn2_norm,
             ffn2_w_gate=ffn2_w_gate, ffn2_w_up=ffn2_w_up, ffn2_w_down=ffn2_w_down)
    layers = [_prep_layer(l, p) for l in range(w_in.shape[0])]
    gd = w_fourier.shape[1] // N_FOURIER_GROUPS
    cos_c, sin_c = _dft_tables(gd)
    cs = jnp.concatenate([cos_c, sin_c], axis=1).astype(BF16)
    final_g = final_norm[None]
    return (_trunk(x_prompt, layers, final_g, cs), _trunk(x_sample, layers, final_g, cs))
```

```python
import functools
import math

import jax
import jax.numpy as jnp
from jax import lax
from jax.experimental import pallas as pl
from jax.experimental.pallas import tpu as pltpu

N_HEADS = 16
QK_NOPE_DIM = 128
QK_ROPE_DIM = 64
V_HEAD_DIM = 128
N_FOURIER_GROUPS = 4
N_BRANCHES = 2
ROPE_THETA = 10000.0
NORM_EPS = 1e-6

LANES = 128
QK_PAD_DIM = 2 * LANES
VT_ROWS = V_HEAD_DIM + 16
ATTN_KEY_CHUNKS = 4
FFN_TILE = 512
CAST_ROWS = 256
DFT_ROW_SPLIT = 64
VMEM_LIMIT_BYTES = 56 * 1024 * 1024

F32 = jnp.float32
BF16 = jnp.bfloat16
_NT = (((1,), (1,)), ((), ()))


def _params(*semantics):
    return pltpu.CompilerParams(dimension_semantics=semantics, vmem_limit_bytes=VMEM_LIMIT_BYTES)


def _resident(shape):
    zeros = (0,) * len(shape)
    return pl.BlockSpec(shape, lambda *_: zeros, pipeline_mode=pl.Buffered(1))


def _rms(x):
    return x * lax.rsqrt(jnp.mean(x * x, axis=-1, keepdims=True) + NORM_EPS)


def _dot(a, b):
    return jnp.dot(a, b, preferred_element_type=F32)


def _cast_kernel(w_ref, o_ref):
    o_ref[...] = w_ref[...].astype(BF16)


def _cast_bf16(w, *, tr=CAST_ROWS):
    lead, r, c = w.shape
    rows = lead * r
    tr = min(tr, rows)
    assert rows % tr == 0
    out = pl.pallas_call(
        _cast_kernel,
        out_shape=jax.ShapeDtypeStruct((rows, c), BF16),
        grid=(rows // tr,),
        in_specs=[pl.BlockSpec((tr, c), lambda i: (i, 0))],
        out_specs=pl.BlockSpec((tr, c), lambda i: (i, 0)),
        compiler_params=_params("parallel"),
        name="cast_bf16",
    )(w.reshape(rows, c))
    return out.reshape(lead, r, c)


def _ffn_kernel(x_ref, g_ref, wg_ref, wu_ref, wd_ref, *rest, n_ff, final):
    if final:
        fg_ref, o_ref, h_ref = rest
    else:
        o_ref, h_ref = rest
    f = pl.program_id(1)

    @pl.when(f == 0)
    def _():
        x = x_ref[...]
        h_ref[...] = (_rms(x) * g_ref[...]).astype(BF16)
        o_ref[...] = x

    h = h_ref[...]
    a = _dot(h, wg_ref[...])
    u = _dot(h, wu_ref[...])
    act = (0.5 * (a * jax.nn.sigmoid(a)) * u).astype(BF16)
    o_ref[...] += _dot(act, wd_ref[...])

    if final:
        @pl.when(f == n_ff - 1)
        def _():
            o_ref[...] = _rms(o_ref[...]) * fg_ref[...]


def _ffn(x, g, wg, wu, wd, layer, final_g=None, *, tm=512, tf=FFN_TILE):
    t, d = x.shape
    d_ff = wg.shape[2]
    tm, tf = min(tm, t), min(tf, d_ff)
    n_ff = d_ff // tf
    assert t % tm == 0 and d_ff % tf == 0
    final = final_g is not None
    in_specs = [
        pl.BlockSpec((tm, d), lambda i, f: (i, 0)),
        _resident((1, d)),
        pl.BlockSpec((None, d, tf), lambda i, f: (layer, 0, f)),
        pl.BlockSpec((None, d, tf), lambda i, f: (layer, 0, f)),
        pl.BlockSpec((None, tf, d), lambda i, f: (layer, f, 0)),
    ]
    args = [x, g, wg, wu, wd]
    if final:
        in_specs.append(_resident((1, d)))
        args.append(final_g)
    return pl.pallas_call(
        functools.partial(_ffn_kernel, n_ff=n_ff, final=final),
        out_shape=jax.ShapeDtypeStruct((t, d), F32),
        grid=(t // tm, n_ff),
        in_specs=in_specs,
        out_specs=pl.BlockSpec((tm, d), lambda i, f: (i, 0)),
        scratch_shapes=[pltpu.VMEM((tm, d), BF16)],
        compiler_params=_params("parallel", "arbitrary"),
        name="ffn_final" if final else "ffn",
    )(*args)


def _fourier_in_kernel(x_ref, g_ref, wf_ref, cs_ref, o_ref, *, gd, n_groups):
    h = (_rms(x_ref[...]) * g_ref[...]).astype(BF16)
    u = _dot(h, wf_ref[...]).astype(BF16)
    cs = cs_ref[...]
    for grp in range(n_groups):
        cols = slice(grp * gd, (grp + 1) * gd)
        a = _dot(u[:, cols], cs)
        o_ref[0, :, cols] = a[:, :gd].astype(BF16)
        o_ref[1, :, cols] = a[:, gd:].astype(BF16)


def _fourier_in(x, g, wf, cs, batch, seq, *, tm=512):
    t, d = x.shape
    fw = wf.shape[1]
    gd = fw // N_FOURIER_GROUPS
    tm = min(tm, seq)
    tps = seq // tm
    assert seq % tm == 0
    return pl.pallas_call(
        functools.partial(_fourier_in_kernel, gd=gd, n_groups=N_FOURIER_GROUPS),
        out_shape=jax.ShapeDtypeStruct((batch, 2, seq, fw), BF16),
        grid=(t // tm,),
        in_specs=[
            pl.BlockSpec((tm, d), lambda i: (i, 0)),
            _resident((1, d)),
            _resident((d, fw)),
            _resident((gd, 2 * gd)),
        ],
        out_specs=pl.BlockSpec((None, 2, tm, fw), lambda i: (i // tps, 0, i % tps, 0)),
        compiler_params=_params("parallel"),
        name="fourier_in",
    )(x, g, wf, cs)


def _seq_dft_kernel(f_ref, r_ref, o_ref):
    o_ref[...] = _dot(f_ref[...], r_ref[...]).astype(BF16)


def _seq_dft(fmat, r, *, tm=512, tn=512):
    batch, s2, fw = r.shape
    seq = s2 // 2
    tm, tn = min(tm, seq), min(tn, fw)
    tps = seq // tm
    assert seq % tm == 0 and fw % tn == 0
    return pl.pallas_call(
        _seq_dft_kernel,
        out_shape=jax.ShapeDtypeStruct((batch * seq, fw), BF16),
        grid=(batch, fw // tn, tps),
        in_specs=[
            pl.BlockSpec((tm, s2), lambda b, n, i: (i, 0)),
            pl.BlockSpec((None, s2, tn), lambda b, n, i: (b, 0, n)),
        ],
        out_specs=pl.BlockSpec((tm, tn), lambda b, n, i: (b * tps + i, n)),
        compiler_params=_params("parallel", "parallel", "arbitrary"),
        name="seq_dft",
    )(fmat, r)


def _latent_kernel(x_ref, g_ref, wlat_ref, qg_ref, kvg_ref, wqn_ref, wqr_ref, wqrr_ref, wk_ref, wvt_ref,
                   cos_ref, sin_ref, q_ref, k_ref, vt_ref, *, ql, kl, n_heads, dv, scale):
    h = (_rms(x_ref[...]) * g_ref[...]).astype(BF16)
    lat = _dot(h, wlat_ref[...])
    cq = (_rms(lat[:, :ql]) * qg_ref[...]).astype(BF16)
    ckv = (_rms(lat[:, ql:ql + kl]) * kvg_ref[...]).astype(BF16)
    cos = cos_ref[...]
    sin = sin_ref[...]
    kr = lat[:, ql + kl:ql + kl + LANES]
    krr = lat[:, ql + kl + LANES:ql + kl + 2 * LANES]
    k_rope = (kr * cos + krr * sin).astype(BF16)
    qn = _dot(cq, wqn_ref[...]) * scale
    qa = _dot(cq, wqr_ref[...])
    qb = _dot(cq, wqrr_ref[...])
    kn = _dot(ckv, wk_ref[...]).astype(BF16)
    vt = lax.dot_general(wvt_ref[...], ckv, _NT, preferred_element_type=F32).astype(BF16)
    ones = jnp.ones((VT_ROWS - dv, vt.shape[1]), BF16)
    for hd in range(n_heads):
        src = slice(hd * LANES, (hd + 1) * LANES)
        lo = slice(hd * QK_PAD_DIM, hd * QK_PAD_DIM + LANES)
        hi = slice(hd * QK_PAD_DIM + LANES, (hd + 1) * QK_PAD_DIM)
        q_ref[:, lo] = qn[:, src].astype(BF16)
        q_ref[:, hi] = ((qa[:, src] * cos + qb[:, src] * sin) * scale).astype(BF16)
        k_ref[:, lo] = kn[:, src]
        k_ref[:, hi] = k_rope
        vt_ref[hd, :dv, :] = vt[hd * dv:(hd + 1) * dv, :]
        vt_ref[hd, dv:, :] = ones


def _latent(x, g, wlat, qg, kvg, wqn, wqr, wqrr, wk, wvt, cos, sin, batch, seq, *, tm=256):
    t, d = x.shape
    ql, kl = qg.shape[1], kvg.shape[1]
    hn = wqn.shape[1]
    n_heads = hn // LANES
    dv = wvt.shape[0] // n_heads
    tm = min(tm, seq)
    tps = seq // tm
    assert seq % tm == 0
    scale = float(QK_NOPE_DIM + QK_ROPE_DIM) ** -0.5 * math.log2(math.e)
    row = lambda i: (i, 0)
    return pl.pallas_call(
        functools.partial(_latent_kernel, ql=ql, kl=kl, n_heads=n_heads, dv=dv, scale=scale),
        out_shape=(
            jax.ShapeDtypeStruct((t, n_heads * QK_PAD_DIM), BF16),
            jax.ShapeDtypeStruct((t, n_heads * QK_PAD_DIM), BF16),
            jax.ShapeDtypeStruct((batch, n_heads, VT_ROWS, seq), BF16),
        ),
        grid=(t // tm,),
        in_specs=[
            pl.BlockSpec((tm, d), row),
            _resident((1, d)),
            _resident(wlat.shape),
            _resident((1, ql)),
            _resident((1, kl)),
            _resident(wqn.shape),
            _resident(wqr.shape),
            _resident(wqrr.shape),
            _resident(wk.shape),
            _resident(wvt.shape),
            pl.BlockSpec((tm, LANES), lambda i: (i % tps, 0)),
            pl.BlockSpec((tm, LANES), lambda i: (i % tps, 0)),
        ],
        out_specs=(
            pl.BlockSpec((tm, n_heads * QK_PAD_DIM), row),
            pl.BlockSpec((tm, n_heads * QK_PAD_DIM), row),
            pl.BlockSpec((None, n_heads, VT_ROWS, tm), lambda i: (i // tps, 0, 0, i % tps)),
        ),
        compiler_params=_params("parallel"),
        name="latent",
    )(x, g, wlat, qg, kvg, wqn, wqr, wqrr, wk, wvt, cos, sin)


def _attn_kernel(q_ref, k_ref, vt_ref, o_ref, st_ref, pt_ref, *, n_sub, tsub, kc, dv):
    n_chunks = k_ref.shape[0] // kc
    col_max = [None] * n_sub
    for t in range(-1, n_sub + 1):
        run_max = None
        ot = None
        for c in range(n_chunks):
            rows = slice(c * kc, (c + 1) * kc)
            if t + 1 < n_sub:
                q = q_ref[(t + 1) * tsub:(t + 2) * tsub, :]
                s = lax.dot_general(k_ref[rows, :], q, _NT, preferred_element_type=F32)
                st_ref[(t + 1) % 2, rows, :] = s
                chunk_max = jnp.max(s, axis=0, keepdims=True)
                run_max = chunk_max if run_max is None else jnp.maximum(run_max, chunk_max)
            if 0 <= t < n_sub:
                pt_ref[t % 2, rows, :] = jnp.exp2(st_ref[t % 2, rows, :] - col_max[t]).astype(BF16)
            if t >= 1:
                part = _dot(vt_ref[:, rows], pt_ref[(t - 1) % 2, rows, :])
                ot = part if ot is None else ot + part
        if t + 1 < n_sub:
            col_max[t + 1] = run_max
        if t >= 1:
            o = ot[:dv] / ot[dv:dv + 1]
            o_ref[(t - 1) * tsub:t * tsub, :] = o.T.astype(BF16)


def _attention(q, k, vt, batch, seq, *, tq=2048, tsub=512, kc=None):
    t = q.shape[0]
    n_heads = q.shape[1] // QK_PAD_DIM
    dv = V_HEAD_DIM
    tq = min(tq, seq)
    tsub = min(tsub, tq)
    kc = seq // ATTN_KEY_CHUNKS if kc is None else min(kc, seq)
    tps = seq // tq
    assert seq % tq == 0 and tq % tsub == 0 and seq % kc == 0
    return pl.pallas_call(
        functools.partial(_attn_kernel, n_sub=tq // tsub, tsub=tsub, kc=kc, dv=dv),
        out_shape=jax.ShapeDtypeStruct((t, n_heads * dv), BF16),
        grid=(batch, n_heads, tps),
        in_specs=[
            pl.BlockSpec((tq, QK_PAD_DIM), lambda b, h, i: (b * tps + i, h)),
            pl.BlockSpec((seq, QK_PAD_DIM), lambda b, h, i: (b, h)),
            pl.BlockSpec((None, None, VT_ROWS, seq), lambda b, h, i: (b, h, 0, 0)),
        ],
        out_specs=pl.BlockSpec((tq, dv), lambda b, h, i: (b * tps + i, h)),
        scratch_shapes=[pltpu.VMEM((2, seq, tsub), F32), pltpu.VMEM((2, seq, tsub), BF16)],
        compiler_params=_params("parallel", "parallel", "arbitrary"),
        name="attention",
    )(q, k, vt)


def _gate_kernel(x_ref, g_ref, w_ref, b_ref, o_ref):
    h = (_rms(x_ref[...]) * g_ref[...]).astype(BF16)
    o_ref[...] = jax.nn.sigmoid(_dot(h, w_ref[...]) + b_ref[...]).astype(BF16)


def _gates(x, g, w, b, *, tm=512):
    t, d = x.shape
    n = w.shape[1]
    tm = min(tm, t)
    assert t % tm == 0
    return pl.pallas_call(
        _gate_kernel,
        out_shape=jax.ShapeDtypeStruct((t, n), BF16),
        grid=(t // tm,),
        in_specs=[
            pl.BlockSpec((tm, d), lambda i: (i, 0)),
            _resident((1, d)),
            _resident((d, n)),
            _resident((1, n)),
        ],
        out_specs=pl.BlockSpec((tm, n), lambda i: (i, 0)),
        compiler_params=_params("parallel"),
        name="gates",
    )(x, g, w, b)


def _merge_kernel(x_ref, mix_ref, att_ref, gate_ref, wf_ref, wo_ref, wout_ref, o_ref, *, d):
    ya = _dot(mix_ref[...], wf_ref[...])
    yb = _dot(att_ref[...], wo_ref[...])
    m = gate_ref[:, :d].astype(F32) * ya + gate_ref[:, d:].astype(F32) * yb
    o_ref[...] = x_ref[...] + _dot(m.astype(BF16), wout_ref[...])


def _merge(x, mixed, att, gates, wf, wo, wout, *, tm=256):
    t, d = x.shape
    tm = min(tm, t)
    assert t % tm == 0
    row = lambda i: (i, 0)
    return pl.pallas_call(
        functools.partial(_merge_kernel, d=d),
        out_shape=jax.ShapeDtypeStruct((t, d), F32),
        grid=(t // tm,),
        in_specs=[
            pl.BlockSpec((tm, d), row),
            pl.BlockSpec((tm, mixed.shape[1]), row),
            pl.BlockSpec((tm, att.shape[1]), row),
            pl.BlockSpec((tm, gates.shape[1]), row),
            _resident(wf.shape),
            _resident(wo.shape),
            _resident(wout.shape),
        ],
        out_specs=pl.BlockSpec((tm, d), row),
        compiler_params=_params("parallel"),
        name="merge",
    )(x, mixed, att, gates, wf, wo, wout)


def _dft_tables(n):
    scale = n ** -0.5

    def trig(rows, k):
        ang = ((rows[:, None] * k[None, :]) % n).astype(F32) * (2.0 * math.pi / n)
        return jnp.cos(ang), jnp.sin(ang)

    k = jnp.arange(n, dtype=jnp.int32)
    m = DFT_ROW_SPLIT
    if n % m or n <= m:
        c, s = trig(k, k)
        return c * scale, s * scale
    ca, sa = trig(jnp.arange(n // m, dtype=jnp.int32) * m, k)
    cb, sb = trig(jnp.arange(m, dtype=jnp.int32), k)
    cb, sb = cb * scale, sb * scale
    c = ca[:, None, :] * cb[None, :, :] - sa[:, None, :] * sb[None, :, :]
    s = sa[:, None, :] * cb[None, :, :] + ca[:, None, :] * sb[None, :, :]
    return c.reshape(n, n), s.reshape(n, n)


def _rope_tables(seq):
    half = QK_ROPE_DIM // 2
    inv_freq = 1.0 / (ROPE_THETA ** (jnp.arange(half, dtype=F32) / half))
    ang = jnp.arange(seq, dtype=F32)[:, None] * inv_freq[None, :]
    pad = jnp.zeros((seq, LANES - QK_ROPE_DIM), F32)
    cos = jnp.concatenate([jnp.cos(ang), jnp.cos(ang), pad], axis=1)
    sin = jnp.concatenate([jnp.sin(ang), jnp.sin(ang), pad], axis=1)
    return cos, sin


def _rot_cols(w):
    half = QK_ROPE_DIM // 2
    return jnp.concatenate([-w[..., half:], w[..., :half]], axis=-1)


def _pad_lanes(w):
    return jnp.concatenate([w, jnp.zeros(w.shape[:-1] + (LANES - w.shape[-1],), w.dtype)], axis=-1)


def _prep_layer(l, p):
    ql, kl = p["q_a_norm"].shape[1], p["kv_a_norm"].shape[1]
    fw = p["w_fourier"].shape[1]
    w_in = p["w_in"][l]
    s0, s1, s2, s3 = fw, fw + ql, fw + ql + kl, fw + ql + kl + QK_ROPE_DIM
    w_kr = w_in[:, s2:s3]
    wlat = jnp.concatenate([w_in[:, s0:s2], _pad_lanes(w_kr), _pad_lanes(_rot_cols(w_kr))], axis=1)
    w_uq = p["w_uq"][l].reshape(ql, N_HEADS, QK_NOPE_DIM + QK_ROPE_DIM)
    w_q_rope = w_uq[:, :, QK_NOPE_DIM:]
    w_ukv = p["w_ukv"][l].reshape(kl, N_HEADS, QK_NOPE_DIM + V_HEAD_DIM)
    bf = lambda a: a.astype(BF16)
    return dict(
        ffn1_norm=p["ffn1_norm"][l][None],
        ffn2_norm=p["ffn2_norm"][l][None],
        mix_norm=p["mix_norm"][l][None],
        wf_in=bf(w_in[:, :s0]),
        wlat=bf(wlat),
        w_gate=bf(w_in[:, s3:]),
        b_gate=p["b_gate"][l][None],
        qg=p["q_a_norm"][l][None],
        kvg=p["kv_a_norm"][l][None],
        wqn=bf(w_uq[:, :, :QK_NOPE_DIM].reshape(ql, N_HEADS * QK_NOPE_DIM)),
        wqr=bf(_pad_lanes(w_q_rope).reshape(ql, N_HEADS * LANES)),
        wqrr=bf(_pad_lanes(_rot_cols(w_q_rope)).reshape(ql, N_HEADS * LANES)),
        wk=bf(w_ukv[:, :, :QK_NOPE_DIM].reshape(kl, N_HEADS * QK_NOPE_DIM)),
        wvt=bf(w_ukv[:, :, QK_NOPE_DIM:].reshape(kl, N_HEADS * V_HEAD_DIM).T),
        w_fourier=bf(p["w_fourier"][l]),
        w_mla_o=bf(p["w_mla_o"][l]),
        w_out=bf(p["w_out"][l]),
    )


def _trunk(x3, layers, ffn_w, final_g, cs):
    batch, seq, d = x3.shape
    x = x3.reshape(batch * seq, d)
    cos_s, sin_s = _dft_tables(seq)
    fmat = jnp.concatenate([cos_s, -sin_s], axis=1).astype(BF16)
    rope_cos, rope_sin = _rope_tables(seq)
    for li, w in enumerate(layers):
        x = _ffn(x, w["ffn1_norm"], *ffn_w["ffn1"], li)
        r = _fourier_in(x, w["mix_norm"], w["wf_in"], cs, batch, seq)
        q, k, vt = _latent(x, w["mix_norm"], w["wlat"], w["qg"], w["kvg"], w["wqn"], w["wqr"], w["wqrr"],
                           w["wk"], w["wvt"], rope_cos, rope_sin, batch, seq)
        gates = _gates(x, w["mix_norm"], w["w_gate"], w["b_gate"])
        mixed = _seq_dft(fmat, r.reshape(batch, 2 * seq, r.shape[-1]))
        att = _attention(q, k, vt, batch, seq)
        x = _merge(x, mixed, att, gates, w["w_fourier"], w["w_mla_o"], w["w_out"])
        x = _ffn(x, w["ffn2_norm"], *ffn_w["ffn2"], li, final_g=final_g if li == len(layers) - 1 else None)
    return x.reshape(batch, seq, d)


def kernel(x_prompt, x_sample, ffn1_norm, ffn1_w_gate, ffn1_w_up, ffn1_w_down, mix_norm, w_in, b_gate, q_a_norm, kv_a_norm, w_uq, w_ukv, w_fourier, w_mla_o, w_out, ffn2_norm, ffn2_w_gate, ffn2_w_up, ffn2_w_down, final_norm):
    p = dict(ffn1_norm=ffn1_norm, ffn1_w_gate=ffn1_w_gate, ffn1_w_up=ffn1_w_up, ffn1_w_down=ffn1_w_down,
             mix_norm=mix_norm, w_in=w_in, b_gate=b_gate, q_a_norm=q_a_norm, kv_a_norm=kv_a_norm, w_uq=w_uq,
             w_ukv=w_ukv, w_fourier=w_fourier, w_mla_o=w_mla_o, w_out=w_out, ffn2_norm=ffn2_norm,
             ffn2_w_gate=ffn2_w_gate, ffn2_w_up=ffn2_w_up, ffn2_w_down=ffn2_w_down)
    layers = [_prep_layer(l, p) for l in range(w_in.shape[0])]
    ffn_w = dict(ffn1=tuple(_cast_bf16(w) for w in (ffn1_w_gate, ffn1_w_up, ffn1_w_down)),
                 ffn2=tuple(_cast_bf16(w) for w in (ffn2_w_gate, ffn2_w_up, ffn2_w_down)))
    gd = w_fourier.shape[1] // N_FOURIER_GROUPS
    cos_c, sin_c = _dft_tables(gd)
    cs = jnp.concatenate([cos_c, sin_c], axis=1).astype(BF16)
    final_g = final_norm[None]
    return (_trunk(x_prompt, layers, ffn_w, final_g, cs), _trunk(x_sample, layers, ffn_w, final_g, cs))
```

```python
import functools
import math

import jax
import jax.numpy as jnp
from jax import lax
from jax.experimental import pallas as pl
from jax.experimental.pallas import tpu as pltpu

N_HEADS = 16
QK_NOPE_DIM = 128
QK_ROPE_DIM = 64
V_HEAD_DIM = 128
N_FOURIER_GROUPS = 4
N_BRANCHES = 2
ROPE_THETA = 10000.0
NORM_EPS = 1e-6

LANES = 128
QK_PAD_DIM = 2 * LANES
VT_ROWS = V_HEAD_DIM + 16
ATTN_KEY_CHUNKS = 4
FFN_TILE = 512
CAST_ROWS = 256
DFT_ROW_SPLIT = 64
VMEM_LIMIT_BYTES = 56 * 1024 * 1024

F32 = jnp.float32
BF16 = jnp.bfloat16
_NT = (((1,), (1,)), ((), ()))


def _params(*semantics):
    return pltpu.CompilerParams(dimension_semantics=semantics, vmem_limit_bytes=VMEM_LIMIT_BYTES)


def _resident(shape):
    zeros = (0,) * len(shape)
    return pl.BlockSpec(shape, lambda *_: zeros, pipeline_mode=pl.Buffered(1))


def _rms(x):
    return x * lax.rsqrt(jnp.mean(x * x, axis=-1, keepdims=True) + NORM_EPS)


def _dot(a, b):
    return jnp.dot(a, b, preferred_element_type=F32)


def _cast_kernel(w_ref, o_ref):
    o_ref[...] = w_ref[...].astype(BF16)


def _cast_bf16(w, *, tr=CAST_ROWS):
    lead, r, c = w.shape
    rows = lead * r
    tr = min(tr, rows)
    assert rows % tr == 0
    out = pl.pallas_call(
        _cast_kernel,
        out_shape=jax.ShapeDtypeStruct((rows, c), BF16),
        grid=(rows // tr,),
        in_specs=[pl.BlockSpec((tr, c), lambda i: (i, 0))],
        out_specs=pl.BlockSpec((tr, c), lambda i: (i, 0)),
        compiler_params=_params("parallel"),
        name="cast_bf16",
    )(w.reshape(rows, c))
    return out.reshape(lead, r, c)


def _ffn_kernel(x_ref, g_ref, wg_ref, wu_ref, wd_ref, *rest, n_ff, final):
    if final:
        fg_ref, o_ref, h_ref = rest
    else:
        o_ref, h_ref = rest
    f = pl.program_id(1)

    @pl.when(f == 0)
    def _():
        x = x_ref[...]
        h_ref[...] = (_rms(x) * g_ref[...]).astype(BF16)
        o_ref[...] = x

    h = h_ref[...]
    a = _dot(h, wg_ref[...])
    u = _dot(h, wu_ref[...])
    act = (0.5 * (a * jax.nn.sigmoid(a)) * u).astype(BF16)
    o_ref[...] += _dot(act, wd_ref[...])

    if final:
        @pl.when(f == n_ff - 1)
        def _():
            o_ref[...] = _rms(o_ref[...]) * fg_ref[...]


def _ffn(x, g, wg, wu, wd, layer, final_g=None, *, tm=512, tf=FFN_TILE):
    t, d = x.shape
    d_ff = wg.shape[2]
    tm, tf = min(tm, t), min(tf, d_ff)
    n_ff = d_ff // tf
    assert t % tm == 0 and d_ff % tf == 0
    final = final_g is not None
    in_specs = [
        pl.BlockSpec((tm, d), lambda i, f: (i, 0)),
        _resident((1, d)),
        pl.BlockSpec((None, d, tf), lambda i, f: (layer, 0, f)),
        pl.BlockSpec((None, d, tf), lambda i, f: (layer, 0, f)),
        pl.BlockSpec((None, tf, d), lambda i, f: (layer, f, 0)),
    ]
    args = [x, g, wg, wu, wd]
    if final:
        in_specs.append(_resident((1, d)))
        args.append(final_g)
    return pl.pallas_call(
        functools.partial(_ffn_kernel, n_ff=n_ff, final=final),
        out_shape=jax.ShapeDtypeStruct((t, d), F32),
        grid=(t // tm, n_ff),
        in_specs=in_specs,
        out_specs=pl.BlockSpec((tm, d), lambda i, f: (i, 0)),
        scratch_shapes=[pltpu.VMEM((tm, d), BF16)],
        compiler_params=_params("parallel", "arbitrary"),
        name="ffn_final" if final else "ffn",
    )(*args)


def _fourier_in_kernel(x_ref, g_ref, wf_ref, cs_ref, o_ref, *, gd, n_groups):
    h = (_rms(x_ref[...]) * g_ref[...]).astype(BF16)
    u = _dot(h, wf_ref[...]).astype(BF16)
    cs = cs_ref[...]
    for grp in range(n_groups):
        cols = slice(grp * gd, (grp + 1) * gd)
        a = _dot(u[:, cols], cs)
        o_ref[0, :, cols] = a[:, :gd].astype(BF16)
        o_ref[1, :, cols] = a[:, gd:].astype(BF16)


def _fourier_in(x, g, wf, cs, batch, seq, *, tm=512):
    t, d = x.shape
    fw = wf.shape[1]
    gd = fw // N_FOURIER_GROUPS
    half = seq // 2
    tm = min(tm, half)
    tps = half // tm
    assert seq % 2 == 0 and half % tm == 0
    return pl.pallas_call(
        functools.partial(_fourier_in_kernel, gd=gd, n_groups=N_FOURIER_GROUPS),
        out_shape=jax.ShapeDtypeStruct((batch, 2, 2, half, fw), BF16),
        grid=(t // 2 // tm, 2),
        in_specs=[
            pl.BlockSpec((tm, d), lambda i, p: (i, p)),
            _resident((1, d)),
            _resident((d, fw)),
            _resident((gd, 2 * gd)),
        ],
        out_specs=pl.BlockSpec((None, None, 2, tm, fw), lambda i, p: (i // tps, p, 0, i % tps, 0)),
        compiler_params=_params("parallel", "parallel"),
        name="fourier_in",
    )(x.reshape(t // 2, 2 * d), g, wf, cs)


def _seq_dft_kernel(fe_ref, fo_ref, re_ref, ro_ref, o_ref):
    e = _dot(fe_ref[...], re_ref[...])
    o = _dot(fo_ref[...], ro_ref[...])
    o_ref[0] = (e + o).astype(BF16)
    o_ref[1] = (e - o).astype(BF16)


def _seq_dft(fe, fo, r, *, tm=512, tn=512):
    batch, _, seq, fw = r.shape
    half = seq // 2
    tm, tn = min(tm, half), min(tn, fw)
    assert half % tm == 0 and fw % tn == 0
    out = pl.pallas_call(
        _seq_dft_kernel,
        out_shape=jax.ShapeDtypeStruct((batch, 2, half, fw), BF16),
        grid=(batch, fw // tn, half // tm),
        in_specs=[
            pl.BlockSpec((tm, seq), lambda b, n, i: (i, 0)),
            pl.BlockSpec((tm, seq), lambda b, n, i: (i, 0)),
            pl.BlockSpec((None, None, seq, tn), lambda b, n, i: (b, 0, 0, n)),
            pl.BlockSpec((None, None, seq, tn), lambda b, n, i: (b, 1, 0, n)),
        ],
        out_specs=pl.BlockSpec((None, 2, tm, tn), lambda b, n, i: (b, 0, i, n)),
        compiler_params=_params("parallel", "parallel", "arbitrary"),
        name="seq_dft",
    )(fe, fo, r, r)
    return out.reshape(batch * seq, fw)


def _latent_kernel(x_ref, g_ref, wlat_ref, qg_ref, kvg_ref, wqn_ref, wqr_ref, wqrr_ref, wk_ref, wvt_ref,
                   cos_ref, sin_ref, q_ref, k_ref, vt_ref, *, ql, kl, n_heads, dv, scale):
    h = (_rms(x_ref[...]) * g_ref[...]).astype(BF16)
    lat = _dot(h, wlat_ref[...])
    cq = (_rms(lat[:, :ql]) * qg_ref[...]).astype(BF16)
    ckv = (_rms(lat[:, ql:ql + kl]) * kvg_ref[...]).astype(BF16)
    cos = cos_ref[...]
    sin = sin_ref[...]
    rope = QK_ROPE_DIM
    low = lax.broadcasted_iota(jnp.int32, cos.shape, 1) < rope

    def low_half(x):
        return jnp.where(low, x, 0.0).astype(BF16)

    kk = lat[:, ql + kl:ql + kl + LANES] * jnp.where(low, cos, sin)
    k_rope = low_half(kk + pltpu.roll(kk, rope, 1))
    qn = _dot(cq, wqn_ref[...]) * scale
    qa = _dot(cq, wqr_ref[...])
    qb = _dot(cq, wqrr_ref[...])
    kn = _dot(ckv, wk_ref[...]).astype(BF16)
    vt = lax.dot_general(wvt_ref[...], ckv, _NT, preferred_element_type=F32).astype(BF16)
    ones = jnp.ones((VT_ROWS - dv, vt.shape[1]), BF16)
    for hd in range(n_heads):
        src = slice(hd * LANES, (hd + 1) * LANES)
        lo = slice(hd * QK_PAD_DIM, hd * QK_PAD_DIM + LANES)
        hi = slice(hd * QK_PAD_DIM + LANES, (hd + 1) * QK_PAD_DIM)
        pair = slice(hd // 2 * LANES, (hd // 2 + 1) * LANES)
        q_rope = (qa[:, pair] * cos + qb[:, pair] * sin) * scale
        q_ref[:, lo] = qn[:, src].astype(BF16)
        q_ref[:, hi] = low_half(q_rope if hd % 2 == 0 else pltpu.roll(q_rope, rope, 1))
        k_ref[:, lo] = kn[:, src]
        k_ref[:, hi] = k_rope
        vt_ref[hd, :dv, :] = vt[hd * dv:(hd + 1) * dv, :]
        vt_ref[hd, dv:, :] = ones


def _latent(x, g, wlat, qg, kvg, wqn, wqr, wqrr, wk, wvt, cos, sin, batch, seq, *, tm=256):
    t, d = x.shape
    ql, kl = qg.shape[1], kvg.shape[1]
    hn = wqn.shape[1]
    n_heads = hn // LANES
    assert n_heads % 2 == 0 and 2 * QK_ROPE_DIM == LANES
    dv = wvt.shape[0] // n_heads
    tm = min(tm, seq)
    tps = seq // tm
    assert seq % tm == 0
    scale = float(QK_NOPE_DIM + QK_ROPE_DIM) ** -0.5 * math.log2(math.e)
    row = lambda i: (i, 0)
    return pl.pallas_call(
        functools.partial(_latent_kernel, ql=ql, kl=kl, n_heads=n_heads, dv=dv, scale=scale),
        out_shape=(
            jax.ShapeDtypeStruct((t, n_heads * QK_PAD_DIM), BF16),
            jax.ShapeDtypeStruct((t, n_heads * QK_PAD_DIM), BF16),
            jax.ShapeDtypeStruct((batch, n_heads, VT_ROWS, seq), BF16),
        ),
        grid=(t // tm,),
        in_specs=[
            pl.BlockSpec((tm, d), row),
            _resident((1, d)),
            _resident(wlat.shape),
            _resident((1, ql)),
            _resident((1, kl)),
            _resident(wqn.shape),
            _resident(wqr.shape),
            _resident(wqrr.shape),
            _resident(wk.shape),
            _resident(wvt.shape),
            pl.BlockSpec((tm, LANES), lambda i: (i % tps, 0)),
            pl.BlockSpec((tm, LANES), lambda i: (i % tps, 0)),
        ],
        out_specs=(
            pl.BlockSpec((tm, n_heads * QK_PAD_DIM), row),
            pl.BlockSpec((tm, n_heads * QK_PAD_DIM), row),
            pl.BlockSpec((None, n_heads, VT_ROWS, tm), lambda i: (i // tps, 0, 0, i % tps)),
        ),
        compiler_params=_params("parallel"),
        name="latent",
    )(x, g, wlat, qg, kvg, wqn, wqr, wqrr, wk, wvt, cos, sin)


def _attn_kernel(q_ref, k_ref, vt_ref, o_ref, st_ref, pt_ref, *, n_sub, tsub, kc, dv):
    n_chunks = k_ref.shape[0] // kc
    col_max = [None] * n_sub
    for t in range(-1, n_sub + 1):
        run_max = None
        ot = None
        for c in range(n_chunks):
            rows = slice(c * kc, (c + 1) * kc)
            if t + 1 < n_sub:
                q = q_ref[(t + 1) * tsub:(t + 2) * tsub, :]
                s = lax.dot_general(k_ref[rows, :], q, _NT, preferred_element_type=F32)
                st_ref[(t + 1) % 2, rows, :] = s
                chunk_max = jnp.max(s, axis=0, keepdims=True)
                run_max = chunk_max if run_max is None else jnp.maximum(run_max, chunk_max)
            if 0 <= t < n_sub:
                pt_ref[t % 2, rows, :] = jnp.exp2(st_ref[t % 2, rows, :] - col_max[t]).astype(BF16)
            if t >= 1:
                part = _dot(vt_ref[:, rows], pt_ref[(t - 1) % 2, rows, :])
                ot = part if ot is None else ot + part
        if t + 1 < n_sub:
            col_max[t + 1] = run_max
        if t >= 1:
            o = ot[:dv] / ot[dv:dv + 1]
            o_ref[(t - 1) * tsub:t * tsub, :] = o.T.astype(BF16)


def _attention(q, k, vt, batch, seq, *, tq=2048, tsub=512, kc=None):
    t = q.shape[0]
    n_heads = q.shape[1] // QK_PAD_DIM
    dv = V_HEAD_DIM
    tq = min(tq, seq)
    tsub = min(tsub, tq)
    kc = seq // ATTN_KEY_CHUNKS if kc is None else min(kc, seq)
    tps = seq // tq
    assert seq % tq == 0 and tq % tsub == 0 and seq % kc == 0
    return pl.pallas_call(
        functools.partial(_attn_kernel, n_sub=tq // tsub, tsub=tsub, kc=kc, dv=dv),
        out_shape=jax.ShapeDtypeStruct((t, n_heads * dv), BF16),
        grid=(batch, n_heads, tps),
        in_specs=[
            pl.BlockSpec((tq, QK_PAD_DIM), lambda b, h, i: (b * tps + i, h)),
            pl.BlockSpec((seq, QK_PAD_DIM), lambda b, h, i: (b, h)),
            pl.BlockSpec((None, None, VT_ROWS, seq), lambda b, h, i: (b, h, 0, 0)),
        ],
        out_specs=pl.BlockSpec((tq, dv), lambda b, h, i: (b * tps + i, h)),
        scratch_shapes=[pltpu.VMEM((2, seq, tsub), F32), pltpu.VMEM((2, seq, tsub), BF16)],
        compiler_params=_params("parallel", "parallel", "arbitrary"),
        name="attention",
    )(q, k, vt)


def _gate_kernel(x_ref, g_ref, w_ref, b_ref, o_ref):
    h = (_rms(x_ref[...]) * g_ref[...]).astype(BF16)
    o_ref[...] = jax.nn.sigmoid(_dot(h, w_ref[...]) + b_ref[...]).astype(BF16)


def _gates(x, g, w, b, *, tm=512):
    t, d = x.shape
    n = w.shape[1]
    tm = min(tm, t)
    assert t % tm == 0
    return pl.pallas_call(
        _gate_kernel,
        out_shape=jax.ShapeDtypeStruct((t, n), BF16),
        grid=(t // tm,),
        in_specs=[
            pl.BlockSpec((tm, d), lambda i: (i, 0)),
            _resident((1, d)),
            _resident((d, n)),
            _resident((1, n)),
        ],
        out_specs=pl.BlockSpec((tm, n), lambda i: (i, 0)),
        compiler_params=_params("parallel"),
        name="gates",
    )(x, g, w, b)


def _merge_kernel(x_ref, mix_ref, att_ref, gate_ref, wf_ref, wo_ref, wout_ref, o_ref, *, d):
    ya = _dot(mix_ref[...], wf_ref[...])
    yb = _dot(att_ref[...], wo_ref[...])
    m = gate_ref[:, :d].astype(F32) * ya + gate_ref[:, d:].astype(F32) * yb
    o_ref[...] = x_ref[...] + _dot(m.astype(BF16), wout_ref[...])


def _merge(x, mixed, att, gates, wf, wo, wout, *, tm=256):
    t, d = x.shape
    tm = min(tm, t)
    assert t % tm == 0
    row = lambda i: (i, 0)
    return pl.pallas_call(
        functools.partial(_merge_kernel, d=d),
        out_shape=jax.ShapeDtypeStruct((t, d), F32),
        grid=(t // tm,),
        in_specs=[
            pl.BlockSpec((tm, d), row),
            pl.BlockSpec((tm, mixed.shape[1]), row),
            pl.BlockSpec((tm, att.shape[1]), row),
            pl.BlockSpec((tm, gates.shape[1]), row),
            _resident(wf.shape),
            _resident(wo.shape),
            _resident(wout.shape),
        ],
        out_specs=pl.BlockSpec((tm, d), row),
        compiler_params=_params("parallel"),
        name="merge",
    )(x, mixed, att, gates, wf, wo, wout)


def _dft_tables(n, scale):
    def trig(rows, k):
        ang = ((rows[:, None] * k[None, :]) % n).astype(F32) * (2.0 * math.pi / n)
        return jnp.cos(ang), jnp.sin(ang)

    k = jnp.arange(n, dtype=jnp.int32)
    m = DFT_ROW_SPLIT
    if n % m or n <= m:
        c, s = trig(k, k)
        return c * scale, s * scale
    ca, sa = trig(jnp.arange(n // m, dtype=jnp.int32) * m, k)
    cb, sb = trig(jnp.arange(m, dtype=jnp.int32), k)
    cb, sb = cb * scale, sb * scale
    c = ca[:, None, :] * cb[None, :, :] - sa[:, None, :] * sb[None, :, :]
    s = sa[:, None, :] * cb[None, :, :] + ca[:, None, :] * sb[None, :, :]
    return c.reshape(n, n), s.reshape(n, n)


def _seq_dft_matrices(seq):
    half = seq // 2
    c, s = _dft_tables(half, seq ** -0.5)
    phi = jnp.arange(half, dtype=F32)[:, None] * (2.0 * math.pi / seq)
    co, so = c * jnp.cos(phi) - s * jnp.sin(phi), s * jnp.cos(phi) + c * jnp.sin(phi)
    fe = jnp.concatenate([c, -s], axis=1).astype(BF16)
    fo = jnp.concatenate([co, -so], axis=1).astype(BF16)
    return fe, fo


def _rope_tables(seq):
    half = QK_ROPE_DIM // 2
    inv_freq = 1.0 / (ROPE_THETA ** (jnp.arange(half, dtype=F32) / half))
    ang = jnp.arange(seq, dtype=F32)[:, None] * inv_freq[None, :]
    reps = LANES // half
    return jnp.tile(jnp.cos(ang), (1, reps)), jnp.tile(jnp.sin(ang), (1, reps))


def _rot_cols(w):
    half = QK_ROPE_DIM // 2
    return jnp.concatenate([-w[..., half:], w[..., :half]], axis=-1)


def _prep_layer(l, p):
    ql, kl = p["q_a_norm"].shape[1], p["kv_a_norm"].shape[1]
    fw = p["w_fourier"].shape[1]
    w_in = p["w_in"][l]
    s0, s1, s2, s3 = fw, fw + ql, fw + ql + kl, fw + ql + kl + QK_ROPE_DIM
    w_kr = w_in[:, s2:s3]
    wlat = jnp.concatenate([w_in[:, s0:s2], w_kr, _rot_cols(w_kr)], axis=1)
    w_uq = p["w_uq"][l].reshape(ql, N_HEADS, QK_NOPE_DIM + QK_ROPE_DIM)
    w_q_rope = w_uq[:, :, QK_NOPE_DIM:]
    w_ukv = p["w_ukv"][l].reshape(kl, N_HEADS, QK_NOPE_DIM + V_HEAD_DIM)
    bf = lambda a: a.astype(BF16)
    return dict(
        ffn1_norm=p["ffn1_norm"][l][None],
        ffn2_norm=p["ffn2_norm"][l][None],
        mix_norm=p["mix_norm"][l][None],
        wf_in=bf(w_in[:, :s0]),
        wlat=bf(wlat),
        w_gate=bf(w_in[:, s3:]),
        b_gate=p["b_gate"][l][None],
        qg=p["q_a_norm"][l][None],
        kvg=p["kv_a_norm"][l][None],
        wqn=bf(w_uq[:, :, :QK_NOPE_DIM].reshape(ql, N_HEADS * QK_NOPE_DIM)),
        wqr=bf(w_q_rope.reshape(ql, N_HEADS * QK_ROPE_DIM)),
        wqrr=bf(_rot_cols(w_q_rope).reshape(ql, N_HEADS * QK_ROPE_DIM)),
        wk=bf(w_ukv[:, :, :QK_NOPE_DIM].reshape(kl, N_HEADS * QK_NOPE_DIM)),
        wvt=bf(w_ukv[:, :, QK_NOPE_DIM:].reshape(kl, N_HEADS * V_HEAD_DIM).T),
        w_fourier=bf(p["w_fourier"][l]),
        w_mla_o=bf(p["w_mla_o"][l]),
        w_out=bf(p["w_out"][l]),
    )


def _trunk(x3, layers, ffn_w, final_g, cs):
    batch, seq, d = x3.shape
    x = x3.reshape(batch * seq, d)
    fe, fo = _seq_dft_matrices(seq)
    rope_cos, rope_sin = _rope_tables(seq)
    for li, w in enumerate(layers):
        x = _ffn(x, w["ffn1_norm"], *ffn_w["ffn1"], li)
        r = _fourier_in(x, w["mix_norm"], w["wf_in"], cs, batch, seq)
        q, k, vt = _latent(x, w["mix_norm"], w["wlat"], w["qg"], w["kvg"], w["wqn"], w["wqr"], w["wqrr"],
                           w["wk"], w["wvt"], rope_cos, rope_sin, batch, seq)
        gates = _gates(x, w["mix_norm"], w["w_gate"], w["b_gate"])
        mixed = _seq_dft(fe, fo, r.reshape(batch, 2, seq, r.shape[-1]))
        att = _attention(q, k, vt, batch, seq)
        x = _merge(x, mixed, att, gates, w["w_fourier"], w["w_mla_o"], w["w_out"])
        x = _ffn(x, w["ffn2_norm"], *ffn_w["ffn2"], li, final_g=final_g if li == len(layers) - 1 else None)
    return x.reshape(batch, seq, d)


def kernel(x_prompt, x_sample, ffn1_norm, ffn1_w_gate, ffn1_w_up, ffn1_w_down, mix_norm, w_in, b_gate, q_a_norm, kv_a_norm, w_uq, w_ukv, w_fourier, w_mla_o, w_out, ffn2_norm, ffn2_w_gate, ffn2_w_up, ffn2_w_down, final_norm):
    p = dict(ffn1_norm=ffn1_norm, ffn1_w_gate=ffn1_w_gate, ffn1_w_up=ffn1_w_up, ffn1_w_down=ffn1_w_down,
             mix_norm=mix_norm, w_in=w_in, b_gate=b_gate, q_a_norm=q_a_norm, kv_a_norm=kv_a_norm, w_uq=w_uq,
             w_ukv=w_ukv, w_fourier=w_fourier, w_mla_o=w_mla_o, w_out=w_out, ffn2_norm=ffn2_norm,
             ffn2_w_gate=ffn2_w_gate, ffn2_w_up=ffn2_w_up, ffn2_w_down=ffn2_w_down)
    layers = [_prep_layer(l, p) for l in range(w_in.shape[0])]
    ffn_w = dict(ffn1=tuple(_cast_bf16(w) for w in (ffn1_w_gate, ffn1_w_up, ffn1_w_down)),
                 ffn2=tuple(_cast_bf16(w) for w in (ffn2_w_gate, ffn2_w_up, ffn2_w_down)))
    gd = w_fourier.shape[1] // N_FOURIER_GROUPS
    cos_c, sin_c = _dft_tables(gd, gd ** -0.5)
    cs = jnp.concatenate([cos_c, sin_c], axis=1).astype(BF16)
    final_g = final_norm[None]
    return (_trunk(x_prompt, layers, ffn_w, final_g, cs), _trunk(x_sample, layers, ffn_w, final_g, cs))
```

```python
import functools
import math

import jax
import jax.numpy as jnp
from jax import lax
from jax.experimental import pallas as pl
from jax.experimental.pallas import tpu as pltpu

N_HEADS = 16
QK_NOPE_DIM = 128
QK_ROPE_DIM = 64
V_HEAD_DIM = 128
N_FOURIER_GROUPS = 4
N_BRANCHES = 2
ROPE_THETA = 10000.0
NORM_EPS = 1e-6

LANES = 128
QK_PAD_DIM = 2 * LANES
VT_ROWS = V_HEAD_DIM + 16
ATTN_KEY_CHUNKS = 4
FFN_TILE = 512
CAST_ROWS = 256
DFT_ROW_SPLIT = 64
VMEM_LIMIT_BYTES = 56 * 1024 * 1024

F32 = jnp.float32
BF16 = jnp.bfloat16
_NT = (((1,), (1,)), ((), ()))


def _params(*semantics):
    return pltpu.CompilerParams(dimension_semantics=semantics, vmem_limit_bytes=VMEM_LIMIT_BYTES)


def _resident(shape):
    zeros = (0,) * len(shape)
    return pl.BlockSpec(shape, lambda *_: zeros, pipeline_mode=pl.Buffered(1))


def _rms(x):
    return x * lax.rsqrt(jnp.mean(x * x, axis=-1, keepdims=True) + NORM_EPS)


def _dot(a, b):
    return jnp.dot(a, b, preferred_element_type=F32)


def _cast_kernel(w_ref, o_ref):
    o_ref[...] = w_ref[...].astype(BF16)


def _cast_bf16(w, *, tr=CAST_ROWS):
    lead, r, c = w.shape
    rows = lead * r
    tr = min(tr, rows)
    assert rows % tr == 0
    out = pl.pallas_call(
        _cast_kernel,
        out_shape=jax.ShapeDtypeStruct((rows, c), BF16),
        grid=(rows // tr,),
        in_specs=[pl.BlockSpec((tr, c), lambda i: (i, 0))],
        out_specs=pl.BlockSpec((tr, c), lambda i: (i, 0)),
        compiler_params=_params("parallel"),
        name="cast_bf16",
    )(w.reshape(rows, c))
    return out.reshape(lead, r, c)


def _ffn_kernel(x_ref, g_ref, wg_ref, wu_ref, wd_ref, *rest, n_ff, final):
    if final:
        fg_ref, o_ref, h_ref = rest
    else:
        o_ref, h_ref = rest
    f = pl.program_id(1)

    @pl.when(f == 0)
    def _():
        x = x_ref[...]
        h_ref[...] = (_rms(x) * g_ref[...]).astype(BF16)
        o_ref[...] = x

    h = h_ref[...]
    a = _dot(h, wg_ref[...])
    u = _dot(h, wu_ref[...])
    act = (0.5 * (a * jax.nn.sigmoid(a)) * u).astype(BF16)
    o_ref[...] += _dot(act, wd_ref[...])

    if final:
        @pl.when(f == n_ff - 1)
        def _():
            o_ref[...] = _rms(o_ref[...]) * fg_ref[...]


def _ffn(x, g, wg, wu, wd, layer, final_g=None, *, tm=512, tf=FFN_TILE):
    t, d = x.shape
    d_ff = wg.shape[2]
    tm, tf = min(tm, t), min(tf, d_ff)
    n_ff = d_ff // tf
    assert t % tm == 0 and d_ff % tf == 0
    final = final_g is not None
    in_specs = [
        pl.BlockSpec((tm, d), lambda i, f: (i, 0)),
        _resident((1, d)),
        pl.BlockSpec((None, d, tf), lambda i, f: (layer, 0, f)),
        pl.BlockSpec((None, d, tf), lambda i, f: (layer, 0, f)),
        pl.BlockSpec((None, tf, d), lambda i, f: (layer, f, 0)),
    ]
    args = [x, g, wg, wu, wd]
    if final:
        in_specs.append(_resident((1, d)))
        args.append(final_g)
    return pl.pallas_call(
        functools.partial(_ffn_kernel, n_ff=n_ff, final=final),
        out_shape=jax.ShapeDtypeStruct((t, d), F32),
        grid=(t // tm, n_ff),
        in_specs=in_specs,
        out_specs=pl.BlockSpec((tm, d), lambda i, f: (i, 0)),
        scratch_shapes=[pltpu.VMEM((tm, d), BF16)],
        compiler_params=_params("parallel", "arbitrary"),
        name="ffn_final" if final else "ffn",
    )(*args)


def _fourier_in_kernel(x_ref, g_ref, wf_ref, cs_ref, perm_ref, o_ref, *, gd, n_groups):
    h = (_rms(x_ref[...]) * g_ref[...]).astype(BF16)
    u = _dot(h, wf_ref[...]).astype(BF16)
    u = _dot(perm_ref[...], u).astype(BF16)
    half = u.shape[0] // 2
    cs = cs_ref[...]
    for grp in range(n_groups):
        cols = slice(grp * gd, (grp + 1) * gd)
        a = _dot(u[:, cols], cs).astype(BF16)
        for parity in range(2):
            rows = slice(parity * half, (parity + 1) * half)
            o_ref[parity, 0, :, cols] = a[rows, :gd]
            o_ref[parity, 1, :, cols] = a[rows, gd:]


def _fourier_in(x, g, wf, cs, batch, seq, *, tm=512):
    t, d = x.shape
    fw = wf.shape[1]
    gd = fw // N_FOURIER_GROUPS
    tm = min(tm, seq)
    tps = seq // tm
    assert seq % tm == 0 and tm % 2 == 0
    src = jnp.arange(tm, dtype=jnp.int32)
    src = jnp.where(src < tm // 2, 2 * src, 2 * (src - tm // 2) + 1)
    perm = (src[:, None] == jnp.arange(tm, dtype=jnp.int32)[None, :]).astype(BF16)
    return pl.pallas_call(
        functools.partial(_fourier_in_kernel, gd=gd, n_groups=N_FOURIER_GROUPS),
        out_shape=jax.ShapeDtypeStruct((batch, 2, 2, seq // 2, fw), BF16),
        grid=(t // tm,),
        in_specs=[
            pl.BlockSpec((tm, d), lambda i: (i, 0)),
            _resident((1, d)),
            _resident((d, fw)),
            _resident((gd, 2 * gd)),
            _resident((tm, tm)),
        ],
        out_specs=pl.BlockSpec((None, 2, 2, tm // 2, fw), lambda i: (i // tps, 0, 0, i % tps, 0)),
        compiler_params=_params("parallel"),
        name="fourier_in",
    )(x, g, wf, cs, perm)


def _seq_dft_kernel(fe_ref, fo_ref, re_ref, ro_ref, o_ref):
    e = _dot(fe_ref[...], re_ref[...])
    o = _dot(fo_ref[...], ro_ref[...])
    o_ref[0] = (e + o).astype(BF16)
    o_ref[1] = (e - o).astype(BF16)


def _seq_dft(fe, fo, r, *, tm=512, tn=512):
    batch, _, seq, fw = r.shape
    half = seq // 2
    tm, tn = min(tm, half), min(tn, fw)
    assert half % tm == 0 and fw % tn == 0
    out = pl.pallas_call(
        _seq_dft_kernel,
        out_shape=jax.ShapeDtypeStruct((batch, 2, half, fw), BF16),
        grid=(batch, fw // tn, half // tm),
        in_specs=[
            pl.BlockSpec((tm, seq), lambda b, n, i: (i, 0)),
            pl.BlockSpec((tm, seq), lambda b, n, i: (i, 0)),
            pl.BlockSpec((None, None, seq, tn), lambda b, n, i: (b, 0, 0, n)),
            pl.BlockSpec((None, None, seq, tn), lambda b, n, i: (b, 1, 0, n)),
        ],
        out_specs=pl.BlockSpec((None, 2, tm, tn), lambda b, n, i: (b, 0, i, n)),
        compiler_params=_params("parallel", "parallel", "arbitrary"),
        name="seq_dft",
    )(fe, fo, r, r)
    return out.reshape(batch * seq, fw)


def _latent_kernel(x_ref, g_ref, wlat_ref, qg_ref, kvg_ref, wqn_ref, wqr_ref, wqrr_ref, wk_ref, wvt_ref,
                   cos_ref, sin_ref, q_ref, k_ref, vt_ref, *, ql, kl, n_heads, dv, scale):
    h = (_rms(x_ref[...]) * g_ref[...]).astype(BF16)
    lat = _dot(h, wlat_ref[...])
    cq = (_rms(lat[:, :ql]) * qg_ref[...]).astype(BF16)
    ckv = (_rms(lat[:, ql:ql + kl]) * kvg_ref[...]).astype(BF16)
    cos = cos_ref[...]
    sin = sin_ref[...]
    rope = QK_ROPE_DIM
    low = lax.broadcasted_iota(jnp.int32, cos.shape, 1) < rope

    def low_half(x):
        return jnp.where(low, x, 0.0).astype(BF16)

    kk = lat[:, ql + kl:ql + kl + LANES] * jnp.where(low, cos, sin)
    k_rope = low_half(kk + pltpu.roll(kk, rope, 1))
    qn = _dot(cq, wqn_ref[...]) * scale
    qa = _dot(cq, wqr_ref[...])
    qb = _dot(cq, wqrr_ref[...])
    kn = _dot(ckv, wk_ref[...]).astype(BF16)
    vt = lax.dot_general(wvt_ref[...], ckv, _NT, preferred_element_type=F32).astype(BF16)
    ones = jnp.ones((VT_ROWS - dv, vt.shape[1]), BF16)
    for hd in range(n_heads):
        src = slice(hd * LANES, (hd + 1) * LANES)
        lo = slice(hd * QK_PAD_DIM, hd * QK_PAD_DIM + LANES)
        hi = slice(hd * QK_PAD_DIM + LANES, (hd + 1) * QK_PAD_DIM)
        pair = slice(hd // 2 * LANES, (hd // 2 + 1) * LANES)
        q_rope = (qa[:, pair] * cos + qb[:, pair] * sin) * scale
        q_ref[:, lo] = qn[:, src].astype(BF16)
        q_ref[:, hi] = low_half(q_rope if hd % 2 == 0 else pltpu.roll(q_rope, rope, 1))
        k_ref[:, lo] = kn[:, src]
        k_ref[:, hi] = k_rope
        vt_ref[hd, :dv, :] = vt[hd * dv:(hd + 1) * dv, :]
        vt_ref[hd, dv:, :] = ones


def _latent(x, g, wlat, qg, kvg, wqn, wqr, wqrr, wk, wvt, cos, sin, batch, seq, *, tm=256):
    t, d = x.shape
    ql, kl = qg.shape[1], kvg.shape[1]
    hn = wqn.shape[1]
    n_heads = hn // LANES
    assert n_heads % 2 == 0 and 2 * QK_ROPE_DIM == LANES
    dv = wvt.shape[0] // n_heads
    tm = min(tm, seq)
    tps = seq // tm
    assert seq % tm == 0
    scale = float(QK_NOPE_DIM + QK_ROPE_DIM) ** -0.5 * math.log2(math.e)
    row = lambda i: (i, 0)
    return pl.pallas_call(
        functools.partial(_latent_kernel, ql=ql, kl=kl, n_heads=n_heads, dv=dv, scale=scale),
        out_shape=(
            jax.ShapeDtypeStruct((t, n_heads * QK_PAD_DIM), BF16),
            jax.ShapeDtypeStruct((t, n_heads * QK_PAD_DIM), BF16),
            jax.ShapeDtypeStruct((batch, n_heads, VT_ROWS, seq), BF16),
        ),
        grid=(t // tm,),
        in_specs=[
            pl.BlockSpec((tm, d), row),
            _resident((1, d)),
            _resident(wlat.shape),
            _resident((1, ql)),
            _resident((1, kl)),
            _resident(wqn.shape),
            _resident(wqr.shape),
            _resident(wqrr.shape),
            _resident(wk.shape),
            _resident(wvt.shape),
            pl.BlockSpec((tm, LANES), lambda i: (i % tps, 0)),
            pl.BlockSpec((tm, LANES), lambda i: (i % tps, 0)),
        ],
        out_specs=(
            pl.BlockSpec((tm, n_heads * QK_PAD_DIM), row),
            pl.BlockSpec((tm, n_heads * QK_PAD_DIM), row),
            pl.BlockSpec((None, n_heads, VT_ROWS, tm), lambda i: (i // tps, 0, 0, i % tps)),
        ),
        compiler_params=_params("parallel"),
        name="latent",
    )(x, g, wlat, qg, kvg, wqn, wqr, wqrr, wk, wvt, cos, sin)


def _attn_kernel(q_ref, k_ref, vt_ref, o_ref, st_ref, pt_ref, *, n_sub, tsub, kc, dv):
    n_chunks = k_ref.shape[0] // kc
    col_max = [None] * n_sub
    for t in range(-1, n_sub + 1):
        run_max = None
        ot = None
        for c in range(n_chunks):
            rows = slice(c * kc, (c + 1) * kc)
            if t + 1 < n_sub:
                q = q_ref[(t + 1) * tsub:(t + 2) * tsub, :]
                s = lax.dot_general(k_ref[rows, :], q, _NT, preferred_element_type=F32)
                st_ref[(t + 1) % 2, rows, :] = s
                chunk_max = jnp.max(s, axis=0, keepdims=True)
                run_max = chunk_max if run_max is None else jnp.maximum(run_max, chunk_max)
            if 0 <= t < n_sub:
                pt_ref[t % 2, rows, :] = jnp.exp2(st_ref[t % 2, rows, :] - col_max[t]).astype(BF16)
            if t >= 1:
                part = _dot(vt_ref[:, rows], pt_ref[(t - 1) % 2, rows, :])
                ot = part if ot is None else ot + part
        if t + 1 < n_sub:
            col_max[t + 1] = run_max
        if t >= 1:
            o = ot[:dv] / ot[dv:dv + 1]
            o_ref[(t - 1) * tsub:t * tsub, :] = o.T.astype(BF16)


def _attention(q, k, vt, batch, seq, *, tq=2048, tsub=512, kc=None):
    t = q.shape[0]
    n_heads = q.shape[1] // QK_PAD_DIM
    dv = V_HEAD_DIM
    tq = min(tq, seq)
    tsub = min(tsub, tq)
    kc = seq // ATTN_KEY_CHUNKS if kc is None else min(kc, seq)
    tps = seq // tq
    assert seq % tq == 0 and tq % tsub == 0 and seq % kc == 0
    return pl.pallas_call(
        functools.partial(_attn_kernel, n_sub=tq // tsub, tsub=tsub, kc=kc, dv=dv),
        out_shape=jax.ShapeDtypeStruct((t, n_heads * dv), BF16),
        grid=(batch, n_heads, tps),
        in_specs=[
            pl.BlockSpec((tq, QK_PAD_DIM), lambda b, h, i: (b * tps + i, h)),
            pl.BlockSpec((seq, QK_PAD_DIM), lambda b, h, i: (b, h)),
            pl.BlockSpec((None, None, VT_ROWS, seq), lambda b, h, i: (b, h, 0, 0)),
        ],
        out_specs=pl.BlockSpec((tq, dv), lambda b, h, i: (b * tps + i, h)),
        scratch_shapes=[pltpu.VMEM((2, seq, tsub), F32), pltpu.VMEM((2, seq, tsub), BF16)],
        compiler_params=_params("parallel", "parallel", "arbitrary"),
        name="attention",
    )(q, k, vt)


def _gate_kernel(x_ref, g_ref, w_ref, b_ref, o_ref):
    h = (_rms(x_ref[...]) * g_ref[...]).astype(BF16)
    o_ref[...] = jax.nn.sigmoid(_dot(h, w_ref[...]) + b_ref[...]).astype(BF16)


def _gates(x, g, w, b, *, tm=512):
    t, d = x.shape
    n = w.shape[1]
    tm = min(tm, t)
    assert t % tm == 0
    return pl.pallas_call(
        _gate_kernel,
        out_shape=jax.ShapeDtypeStruct((t, n), BF16),
        grid=(t // tm,),
        in_specs=[
            pl.BlockSpec((tm, d), lambda i: (i, 0)),
            _resident((1, d)),
            _resident((d, n)),
            _resident((1, n)),
        ],
        out_specs=pl.BlockSpec((tm, n), lambda i: (i, 0)),
        compiler_params=_params("parallel"),
        name="gates",
    )(x, g, w, b)


def _merge_kernel(x_ref, mix_ref, att_ref, gate_ref, wf_ref, wo_ref, wout_ref, o_ref, *, d):
    ya = _dot(mix_ref[...], wf_ref[...])
    yb = _dot(att_ref[...], wo_ref[...])
    m = gate_ref[:, :d].astype(F32) * ya + gate_ref[:, d:].astype(F32) * yb
    o_ref[...] = x_ref[...] + _dot(m.astype(BF16), wout_ref[...])


def _merge(x, mixed, att, gates, wf, wo, wout, *, tm=256):
    t, d = x.shape
    tm = min(tm, t)
    assert t % tm == 0
    row = lambda i: (i, 0)
    return pl.pallas_call(
        functools.partial(_merge_kernel, d=d),
        out_shape=jax.ShapeDtypeStruct((t, d), F32),
        grid=(t // tm,),
        in_specs=[
            pl.BlockSpec((tm, d), row),
            pl.BlockSpec((tm, mixed.shape[1]), row),
            pl.BlockSpec((tm, att.shape[1]), row),
            pl.BlockSpec((tm, gates.shape[1]), row),
            _resident(wf.shape),
            _resident(wo.shape),
            _resident(wout.shape),
        ],
        out_specs=pl.BlockSpec((tm, d), row),
        compiler_params=_params("parallel"),
        name="merge",
    )(x, mixed, att, gates, wf, wo, wout)


def _dft_tables(n, scale):
    def trig(rows, k):
        ang = ((rows[:, None] * k[None, :]) % n).astype(F32) * (2.0 * math.pi / n)
        return jnp.cos(ang), jnp.sin(ang)

    k = jnp.arange(n, dtype=jnp.int32)
    m = DFT_ROW_SPLIT
    if n % m or n <= m:
        c, s = trig(k, k)
        return c * scale, s * scale
    ca, sa = trig(jnp.arange(n // m, dtype=jnp.int32) * m, k)
    cb, sb = trig(jnp.arange(m, dtype=jnp.int32), k)
    cb, sb = cb * scale, sb * scale
    c = ca[:, None, :] * cb[None, :, :] - sa[:, None, :] * sb[None, :, :]
    s = sa[:, None, :] * cb[None, :, :] + ca[:, None, :] * sb[None, :, :]
    return c.reshape(n, n), s.reshape(n, n)


def _seq_dft_matrices(seq):
    half = seq // 2
    c, s = _dft_tables(half, seq ** -0.5)
    phi = jnp.arange(half, dtype=F32)[:, None] * (2.0 * math.pi / seq)
    co, so = c * jnp.cos(phi) - s * jnp.sin(phi), s * jnp.cos(phi) + c * jnp.sin(phi)
    fe = jnp.concatenate([c, -s], axis=1).astype(BF16)
    fo = jnp.concatenate([co, -so], axis=1).astype(BF16)
    return fe, fo


def _rope_tables(seq):
    half = QK_ROPE_DIM // 2
    inv_freq = 1.0 / (ROPE_THETA ** (jnp.arange(half, dtype=F32) / half))
    ang = jnp.arange(seq, dtype=F32)[:, None] * inv_freq[None, :]
    reps = LANES // half
    return jnp.tile(jnp.cos(ang), (1, reps)), jnp.tile(jnp.sin(ang), (1, reps))


def _rot_cols(w):
    half = QK_ROPE_DIM // 2
    return jnp.concatenate([-w[..., half:], w[..., :half]], axis=-1)


def _prep_layer(l, p):
    ql, kl = p["q_a_norm"].shape[1], p["kv_a_norm"].shape[1]
    fw = p["w_fourier"].shape[1]
    w_in = p["w_in"][l]
    s0, s1, s2, s3 = fw, fw + ql, fw + ql + kl, fw + ql + kl + QK_ROPE_DIM
    w_kr = w_in[:, s2:s3]
    wlat = jnp.concatenate([w_in[:, s0:s2], w_kr, _rot_cols(w_kr)], axis=1)
    w_uq = p["w_uq"][l].reshape(ql, N_HEADS, QK_NOPE_DIM + QK_ROPE_DIM)
    w_q_rope = w_uq[:, :, QK_NOPE_DIM:]
    w_ukv = p["w_ukv"][l].reshape(kl, N_HEADS, QK_NOPE_DIM + V_HEAD_DIM)
    bf = lambda a: a.astype(BF16)
    return dict(
        ffn1_norm=p["ffn1_norm"][l][None],
        ffn2_norm=p["ffn2_norm"][l][None],
        mix_norm=p["mix_norm"][l][None],
        wf_in=bf(w_in[:, :s0]),
        wlat=bf(wlat),
        w_gate=bf(w_in[:, s3:]),
        b_gate=p["b_gate"][l][None],
        qg=p["q_a_norm"][l][None],
        kvg=p["kv_a_norm"][l][None],
        wqn=bf(w_uq[:, :, :QK_NOPE_DIM].reshape(ql, N_HEADS * QK_NOPE_DIM)),
        wqr=bf(w_q_rope.reshape(ql, N_HEADS * QK_ROPE_DIM)),
        wqrr=bf(_rot_cols(w_q_rope).reshape(ql, N_HEADS * QK_ROPE_DIM)),
        wk=bf(w_ukv[:, :, :QK_NOPE_DIM].reshape(kl, N_HEADS * QK_NOPE_DIM)),
        wvt=bf(w_ukv[:, :, QK_NOPE_DIM:].reshape(kl, N_HEADS * V_HEAD_DIM).T),
        w_fourier=bf(p["w_fourier"][l]),
        w_mla_o=bf(p["w_mla_o"][l]),
        w_out=bf(p["w_out"][l]),
    )


def _trunk(x3, layers, ffn_w, final_g, cs):
    batch, seq, d = x3.shape
    x = x3.reshape(batch * seq, d)
    fe, fo = _seq_dft_matrices(seq)
    rope_cos, rope_sin = _rope_tables(seq)
    for li, w in enumerate(layers):
        x = _ffn(x, w["ffn1_norm"], *ffn_w["ffn1"], li)
        r = _fourier_in(x, w["mix_norm"], w["wf_in"], cs, batch, seq)
        q, k, vt = _latent(x, w["mix_norm"], w["wlat"], w["qg"], w["kvg"], w["wqn"], w["wqr"], w["wqrr"],
                           w["wk"], w["wvt"], rope_cos, rope_sin, batch, seq)
        gates = _gates(x, w["mix_norm"], w["w_gate"], w["b_gate"])
        mixed = _seq_dft(fe, fo, r.reshape(batch, 2, seq, r.shape[-1]))
        att = _attention(q, k, vt, batch, seq)
        x = _merge(x, mixed, att, gates, w["w_fourier"], w["w_mla_o"], w["w_out"])
        x = _ffn(x, w["ffn2_norm"], *ffn_w["ffn2"], li, final_g=final_g if li == len(layers) - 1 else None)
    return x.reshape(batch, seq, d)


def kernel(x_prompt, x_sample, ffn1_norm, ffn1_w_gate, ffn1_w_up, ffn1_w_down, mix_norm, w_in, b_gate, q_a_norm, kv_a_norm, w_uq, w_ukv, w_fourier, w_mla_o, w_out, ffn2_norm, ffn2_w_gate, ffn2_w_up, ffn2_w_down, final_norm):
    p = dict(ffn1_norm=ffn1_norm, ffn1_w_gate=ffn1_w_gate, ffn1_w_up=ffn1_w_up, ffn1_w_down=ffn1_w_down,
             mix_norm=mix_norm, w_in=w_in, b_gate=b_gate, q_a_norm=q_a_norm, kv_a_norm=kv_a_norm, w_uq=w_uq,
             w_ukv=w_ukv, w_fourier=w_fourier, w_mla_o=w_mla_o, w_out=w_out, ffn2_norm=ffn2_norm,
             ffn2_w_gate=ffn2_w_gate, ffn2_w_up=ffn2_w_up, ffn2_w_down=ffn2_w_down)
    layers = [_prep_layer(l, p) for l in range(w_in.shape[0])]
    ffn_w = dict(ffn1=tuple(_cast_bf16(w) for w in (ffn1_w_gate, ffn1_w_up, ffn1_w_down)),
                 ffn2=tuple(_cast_bf16(w) for w in (ffn2_w_gate, ffn2_w_up, ffn2_w_down)))
    gd = w_fourier.shape[1] // N_FOURIER_GROUPS
    cos_c, sin_c = _dft_tables(gd, gd ** -0.5)
    cs = jnp.concatenate([cos_c, sin_c], axis=1).astype(BF16)
    final_g = final_norm[None]
    return (_trunk(x_prompt, layers, ffn_w, final_g, cs), _trunk(x_sample, layers, ffn_w, final_g, cs))
```

```python
import functools
import math

import jax
import jax.numpy as jnp
from jax import lax
from jax.experimental import pallas as pl
from jax.experimental.pallas import tpu as pltpu

N_HEADS = 16
QK_NOPE_DIM = 128
QK_ROPE_DIM = 64
V_HEAD_DIM = 128
N_FOURIER_GROUPS = 4
N_BRANCHES = 2
ROPE_THETA = 10000.0
NORM_EPS = 1e-6

LANES = 128
QK_PAD_DIM = 2 * LANES
VT_ROWS = V_HEAD_DIM + 16
ATTN_KEY_CHUNKS = 4
ATTN_PAIRS = 8
FFN_TILE = 512
CAST_ROWS = 256
DFT_ROW_SPLIT = 64
VMEM_LIMIT_BYTES = 56 * 1024 * 1024

F32 = jnp.float32
BF16 = jnp.bfloat16
_NT = (((1,), (1,)), ((), ()))


def _params(*semantics):
    return pltpu.CompilerParams(dimension_semantics=semantics, vmem_limit_bytes=VMEM_LIMIT_BYTES)


def _resident(shape):
    zeros = (0,) * len(shape)
    return pl.BlockSpec(shape, lambda *_: zeros, pipeline_mode=pl.Buffered(1))


def _rms(x):
    return x * lax.rsqrt(jnp.mean(x * x, axis=-1, keepdims=True) + NORM_EPS)


def _dot(a, b):
    return jnp.dot(a, b, preferred_element_type=F32)


def _cast_kernel(w_ref, o_ref):
    o_ref[...] = w_ref[...].astype(BF16)


def _cast_bf16(w, *, tr=CAST_ROWS):
    lead, r, c = w.shape
    rows = lead * r
    tr = min(tr, rows)
    assert rows % tr == 0
    out = pl.pallas_call(
        _cast_kernel,
        out_shape=jax.ShapeDtypeStruct((rows, c), BF16),
        grid=(rows // tr,),
        in_specs=[pl.BlockSpec((tr, c), lambda i: (i, 0))],
        out_specs=pl.BlockSpec((tr, c), lambda i: (i, 0)),
        compiler_params=_params("parallel"),
        name="cast_bf16",
    )(w.reshape(rows, c))
    return out.reshape(lead, r, c)


def _ffn_kernel(x_ref, g_ref, wg_ref, wu_ref, wd_ref, *rest, n_ff, final):
    if final:
        fg_ref, o_ref, h_ref = rest
    else:
        o_ref, h_ref = rest
    f = pl.program_id(1)

    @pl.when(f == 0)
    def _():
        x = x_ref[...]
        h_ref[...] = (_rms(x) * g_ref[...]).astype(BF16)
        o_ref[...] = x

    h = h_ref[...]
    a = _dot(h, wg_ref[...])
    u = _dot(h, wu_ref[...])
    act = (0.5 * (a * jax.nn.sigmoid(a)) * u).astype(BF16)
    o_ref[...] += _dot(act, wd_ref[...])

    if final:
        @pl.when(f == n_ff - 1)
        def _():
            o_ref[...] = _rms(o_ref[...]) * fg_ref[...]


def _ffn(x, g, wg, wu, wd, layer, final_g=None, *, tm=512, tf=FFN_TILE):
    t, d = x.shape
    d_ff = wg.shape[2]
    tm, tf = min(tm, t), min(tf, d_ff)
    n_ff = d_ff // tf
    assert t % tm == 0 and d_ff % tf == 0
    final = final_g is not None
    in_specs = [
        pl.BlockSpec((tm, d), lambda i, f: (i, 0)),
        _resident((1, d)),
        pl.BlockSpec((None, d, tf), lambda i, f: (layer, 0, f)),
        pl.BlockSpec((None, d, tf), lambda i, f: (layer, 0, f)),
        pl.BlockSpec((None, tf, d), lambda i, f: (layer, f, 0)),
    ]
    args = [x, g, wg, wu, wd]
    if final:
        in_specs.append(_resident((1, d)))
        args.append(final_g)
    return pl.pallas_call(
        functools.partial(_ffn_kernel, n_ff=n_ff, final=final),
        out_shape=jax.ShapeDtypeStruct((t, d), F32),
        grid=(t // tm, n_ff),
        in_specs=in_specs,
        out_specs=pl.BlockSpec((tm, d), lambda i, f: (i, 0)),
        scratch_shapes=[pltpu.VMEM((tm, d), BF16)],
        compiler_params=_params("parallel", "arbitrary"),
        name="ffn_final" if final else "ffn",
    )(*args)


def _fourier_in_kernel(x_ref, g_ref, wf_ref, cs_ref, perm_ref, o_ref, *, gd, n_groups):
    h = (_rms(x_ref[...]) * g_ref[...]).astype(BF16)
    u = _dot(h, wf_ref[...]).astype(BF16)
    u = _dot(perm_ref[...], u).astype(BF16)
    half = u.shape[0] // 2
    cs = cs_ref[...]
    for grp in range(n_groups):
        cols = slice(grp * gd, (grp + 1) * gd)
        a = _dot(u[:, cols], cs).astype(BF16)
        for parity in range(2):
            rows = slice(parity * half, (parity + 1) * half)
            o_ref[parity, 0, :, cols] = a[rows, :gd]
            o_ref[parity, 1, :, cols] = a[rows, gd:]


def _fourier_in(x, g, wf, cs, batch, seq, *, tm=512):
    t, d = x.shape
    fw = wf.shape[1]
    gd = fw // N_FOURIER_GROUPS
    tm = min(tm, seq)
    tps = seq // tm
    assert seq % tm == 0 and tm % 2 == 0
    src = jnp.arange(tm, dtype=jnp.int32)
    src = jnp.where(src < tm // 2, 2 * src, 2 * (src - tm // 2) + 1)
    perm = (src[:, None] == jnp.arange(tm, dtype=jnp.int32)[None, :]).astype(BF16)
    return pl.pallas_call(
        functools.partial(_fourier_in_kernel, gd=gd, n_groups=N_FOURIER_GROUPS),
        out_shape=jax.ShapeDtypeStruct((batch, 2, 2, seq // 2, fw), BF16),
        grid=(t // tm,),
        in_specs=[
            pl.BlockSpec((tm, d), lambda i: (i, 0)),
            _resident((1, d)),
            _resident((d, fw)),
            _resident((gd, 2 * gd)),
            _resident((tm, tm)),
        ],
        out_specs=pl.BlockSpec((None, 2, 2, tm // 2, fw), lambda i: (i // tps, 0, 0, i % tps, 0)),
        compiler_params=_params("parallel"),
        name="fourier_in",
    )(x, g, wf, cs, perm)


def _seq_dft_kernel(fe_ref, fo_ref, re_ref, ro_ref, o_ref):
    e = _dot(fe_ref[...], re_ref[...])
    o = _dot(fo_ref[...], ro_ref[...])
    o_ref[0] = (e + o).astype(BF16)
    o_ref[1] = (e - o).astype(BF16)


def _seq_dft(fe, fo, r, *, tm=256, tn=1024):
    batch, _, seq, fw = r.shape
    half = seq // 2
    tm, tn = min(tm, half), min(tn, fw)
    assert half % tm == 0 and fw % tn == 0
    out = pl.pallas_call(
        _seq_dft_kernel,
        out_shape=jax.ShapeDtypeStruct((batch, 2, half, fw), BF16),
        grid=(batch, fw // tn, half // tm),
        in_specs=[
            pl.BlockSpec((tm, seq), lambda b, n, i: (i, 0)),
            pl.BlockSpec((tm, seq), lambda b, n, i: (i, 0)),
            pl.BlockSpec((None, None, seq, tn), lambda b, n, i: (b, 0, 0, n)),
            pl.BlockSpec((None, None, seq, tn), lambda b, n, i: (b, 1, 0, n)),
        ],
        out_specs=pl.BlockSpec((None, 2, tm, tn), lambda b, n, i: (b, 0, i, n)),
        compiler_params=_params("parallel", "parallel", "arbitrary"),
        name="seq_dft",
    )(fe, fo, r, r)
    return out.reshape(batch * seq, fw)


def _latent_kernel(x_ref, g_ref, wlat_ref, qg_ref, kvg_ref, wqn_ref, wqr_ref, wqrr_ref, wk_ref, wvt_ref,
                   cos_ref, sin_ref, q_ref, k_ref, vt_ref, *, ql, kl, n_heads, dv, scale):
    h = (_rms(x_ref[...]) * g_ref[...]).astype(BF16)
    lat = _dot(h, wlat_ref[...])
    cq = (_rms(lat[:, :ql]) * qg_ref[...]).astype(BF16)
    ckv = (_rms(lat[:, ql:ql + kl]) * kvg_ref[...]).astype(BF16)
    cos = cos_ref[...]
    sin = sin_ref[...]
    rope = QK_ROPE_DIM
    low = lax.broadcasted_iota(jnp.int32, cos.shape, 1) < rope

    def low_half(x):
        return jnp.where(low, x, 0.0).astype(BF16)

    kk = lat[:, ql + kl:ql + kl + LANES] * jnp.where(low, cos, sin)
    k_rope = low_half(kk + pltpu.roll(kk, rope, 1))
    qn = _dot(cq, wqn_ref[...]) * scale
    qa = _dot(cq, wqr_ref[...])
    qb = _dot(cq, wqrr_ref[...])
    kn = _dot(ckv, wk_ref[...]).astype(BF16)
    vt = lax.dot_general(wvt_ref[...], ckv, _NT, preferred_element_type=F32).astype(BF16)
    ones = jnp.ones((VT_ROWS - dv, vt.shape[1]), BF16)
    for hd in range(n_heads):
        src = slice(hd * LANES, (hd + 1) * LANES)
        lo = slice(hd * QK_PAD_DIM, hd * QK_PAD_DIM + LANES)
        hi = slice(hd * QK_PAD_DIM + LANES, (hd + 1) * QK_PAD_DIM)
        pair = slice(hd // 2 * LANES, (hd // 2 + 1) * LANES)
        q_rope = (qa[:, pair] * cos + qb[:, pair] * sin) * scale
        q_ref[:, lo] = qn[:, src].astype(BF16)
        q_ref[:, hi] = low_half(q_rope if hd % 2 == 0 else pltpu.roll(q_rope, rope, 1))
        k_ref[:, lo] = kn[:, src]
        k_ref[:, hi] = k_rope
        vt_ref[hd, :dv, :] = vt[hd * dv:(hd + 1) * dv, :]
        vt_ref[hd, dv:, :] = ones


def _latent(x, g, wlat, qg, kvg, wqn, wqr, wqrr, wk, wvt, cos, sin, batch, seq, *, tm=256):
    t, d = x.shape
    ql, kl = qg.shape[1], kvg.shape[1]
    hn = wqn.shape[1]
    n_heads = hn // LANES
    assert n_heads % 2 == 0 and 2 * QK_ROPE_DIM == LANES
    dv = wvt.shape[0] // n_heads
    tm = min(tm, seq)
    tps = seq // tm
    assert seq % tm == 0
    scale = float(QK_NOPE_DIM + QK_ROPE_DIM) ** -0.5 * math.log2(math.e)
    row = lambda i: (i, 0)
    return pl.pallas_call(
        functools.partial(_latent_kernel, ql=ql, kl=kl, n_heads=n_heads, dv=dv, scale=scale),
        out_shape=(
            jax.ShapeDtypeStruct((t, n_heads * QK_PAD_DIM), BF16),
            jax.ShapeDtypeStruct((t, n_heads * QK_PAD_DIM), BF16),
            jax.ShapeDtypeStruct((batch, n_heads, VT_ROWS, seq), BF16),
        ),
        grid=(t // tm,),
        in_specs=[
            pl.BlockSpec((tm, d), row),
            _resident((1, d)),
            _resident(wlat.shape),
            _resident((1, ql)),
            _resident((1, kl)),
            _resident(wqn.shape),
            _resident(wqr.shape),
            _resident(wqrr.shape),
            _resident(wk.shape),
            _resident(wvt.shape),
            pl.BlockSpec((tm, LANES), lambda i: (i % tps, 0)),
            pl.BlockSpec((tm, LANES), lambda i: (i % tps, 0)),
        ],
        out_specs=(
            pl.BlockSpec((tm, n_heads * QK_PAD_DIM), row),
            pl.BlockSpec((tm, n_heads * QK_PAD_DIM), row),
            pl.BlockSpec((None, n_heads, VT_ROWS, tm), lambda i: (i // tps, 0, 0, i % tps)),
        ),
        compiler_params=_params("parallel"),
        name="latent",
    )(x, g, wlat, qg, kvg, wqn, wqr, wqrr, wk, wvt, cos, sin)


def _attn_kernel(q_ref, k_ref, vt_ref, o_ref, st_ref, pt_ref, *, n_heads, n_sub, tsub, kc, dv):
    n_chunks = k_ref.shape[0] // kc
    pairs = [(h, j) for h in range(n_heads) for j in range(n_sub)]
    col_max = [None] * len(pairs)
    for t in range(-1, len(pairs) + 1):
        run_max = None
        ot = None
        for c in range(n_chunks):
            rows = slice(c * kc, (c + 1) * kc)
            if t + 1 < len(pairs):
                h, j = pairs[t + 1]
                head = slice(h * QK_PAD_DIM, (h + 1) * QK_PAD_DIM)
                s = lax.dot_general(k_ref[rows, head], q_ref[j * tsub:(j + 1) * tsub, head], _NT,
                                    preferred_element_type=F32)
                st_ref[(t + 1) % 2, rows, :] = s
                chunk_max = jnp.max(s, axis=0, keepdims=True)
                run_max = chunk_max if run_max is None else jnp.maximum(run_max, chunk_max)
            if 0 <= t < len(pairs):
                pt_ref[t % 2, rows, :] = jnp.exp2(st_ref[t % 2, rows, :] - col_max[t]).astype(BF16)
            if t >= 1:
                h, j = pairs[t - 1]
                part = _dot(vt_ref[h, :, rows], pt_ref[(t - 1) % 2, rows, :])
                ot = part if ot is None else ot + part
        if t + 1 < len(pairs):
            col_max[t + 1] = run_max
        if t >= 1:
            h, j = pairs[t - 1]
            o = ot[:dv] / ot[dv:dv + 1]
            o_ref[j * tsub:(j + 1) * tsub, h * dv:(h + 1) * dv] = o.T.astype(BF16)


def _attention(q, k, vt, batch, seq, *, tq=4096, tsub=512, kc=None):
    t = q.shape[0]
    n_heads = q.shape[1] // QK_PAD_DIM
    dv = V_HEAD_DIM
    tq = min(tq, seq)
    tsub = min(tsub, tq)
    n_sub = tq // tsub
    hb = max(1, ATTN_PAIRS // n_sub)
    hb = hb if n_heads % hb == 0 else 1
    kc = seq // ATTN_KEY_CHUNKS if kc is None else min(kc, seq)
    tps = seq // tq
    assert seq % tq == 0 and tq % tsub == 0 and seq % kc == 0
    return pl.pallas_call(
        functools.partial(_attn_kernel, n_heads=hb, n_sub=n_sub, tsub=tsub, kc=kc, dv=dv),
        out_shape=jax.ShapeDtypeStruct((t, n_heads * dv), BF16),
        grid=(batch, n_heads // hb, tps),
        in_specs=[
            pl.BlockSpec((tq, hb * QK_PAD_DIM), lambda b, h, i: (b * tps + i, h)),
            pl.BlockSpec((seq, hb * QK_PAD_DIM), lambda b, h, i: (b, h)),
            pl.BlockSpec((None, hb, VT_ROWS, seq), lambda b, h, i: (b, h, 0, 0)),
        ],
        out_specs=pl.BlockSpec((tq, hb * dv), lambda b, h, i: (b * tps + i, h)),
        scratch_shapes=[pltpu.VMEM((2, seq, tsub), F32), pltpu.VMEM((2, seq, tsub), BF16)],
        compiler_params=_params("parallel", "parallel", "arbitrary"),
        name="attention",
    )(q, k, vt)


def _gate_kernel(x_ref, g_ref, w_ref, b_ref, o_ref):
    h = (_rms(x_ref[...]) * g_ref[...]).astype(BF16)
    o_ref[...] = jax.nn.sigmoid(_dot(h, w_ref[...]) + b_ref[...]).astype(BF16)


def _gates(x, g, w, b, *, tm=512):
    t, d = x.shape
    n = w.shape[1]
    tm = min(tm, t)
    assert t % tm == 0
    return pl.pallas_call(
        _gate_kernel,
        out_shape=jax.ShapeDtypeStruct((t, n), BF16),
        grid=(t // tm,),
        in_specs=[
            pl.BlockSpec((tm, d), lambda i: (i, 0)),
            _resident((1, d)),
            _resident((d, n)),
            _resident((1, n)),
        ],
        out_specs=pl.BlockSpec((tm, n), lambda i: (i, 0)),
        compiler_params=_params("parallel"),
        name="gates",
    )(x, g, w, b)


def _merge_kernel(x_ref, mix_ref, att_ref, gate_ref, wf_ref, wo_ref, wout_ref, o_ref, *, d):
    ya = _dot(mix_ref[...], wf_ref[...])
    yb = _dot(att_ref[...], wo_ref[...])
    m = gate_ref[:, :d].astype(F32) * ya + gate_ref[:, d:].astype(F32) * yb
    o_ref[...] = x_ref[...] + _dot(m.astype(BF16), wout_ref[...])


def _merge(x, mixed, att, gates, wf, wo, wout, *, tm=256):
    t, d = x.shape
    tm = min(tm, t)
    assert t % tm == 0
    row = lambda i: (i, 0)
    return pl.pallas_call(
        functools.partial(_merge_kernel, d=d),
        out_shape=jax.ShapeDtypeStruct((t, d), F32),
        grid=(t // tm,),
        in_specs=[
            pl.BlockSpec((tm, d), row),
            pl.BlockSpec((tm, mixed.shape[1]), row),
            pl.BlockSpec((tm, att.shape[1]), row),
            pl.BlockSpec((tm, gates.shape[1]), row),
            _resident(wf.shape),
            _resident(wo.shape),
            _resident(wout.shape),
        ],
        out_specs=pl.BlockSpec((tm, d), row),
        compiler_params=_params("parallel"),
        name="merge",
    )(x, mixed, att, gates, wf, wo, wout)


def _dft_tables(n, scale):
    def trig(rows, k):
        ang = ((rows[:, None] * k[None, :]) % n).astype(F32) * (2.0 * math.pi / n)
        return jnp.cos(ang), jnp.sin(ang)

    k = jnp.arange(n, dtype=jnp.int32)
    m = DFT_ROW_SPLIT
    if n % m or n <= m:
        c, s = trig(k, k)
        return c * scale, s * scale
    ca, sa = trig(jnp.arange(n // m, dtype=jnp.int32) * m, k)
    cb, sb = trig(jnp.arange(m, dtype=jnp.int32), k)
    cb, sb = cb * scale, sb * scale
    c = ca[:, None, :] * cb[None, :, :] - sa[:, None, :] * sb[None, :, :]
    s = sa[:, None, :] * cb[None, :, :] + ca[:, None, :] * sb[None, :, :]
    return c.reshape(n, n), s.reshape(n, n)


def _seq_dft_matrices(seq):
    half = seq // 2
    c, s = _dft_tables(half, seq ** -0.5)
    phi = jnp.arange(half, dtype=F32)[:, None] * (2.0 * math.pi / seq)
    co, so = c * jnp.cos(phi) - s * jnp.sin(phi), s * jnp.cos(phi) + c * jnp.sin(phi)
    fe = jnp.concatenate([c, -s], axis=1).astype(BF16)
    fo = jnp.concatenate([co, -so], axis=1).astype(BF16)
    return fe, fo


def _rope_tables(seq):
    half = QK_ROPE_DIM // 2
    inv_freq = 1.0 / (ROPE_THETA ** (jnp.arange(half, dtype=F32) / half))
    ang = jnp.arange(seq, dtype=F32)[:, None] * inv_freq[None, :]
    reps = LANES // half
    return jnp.tile(jnp.cos(ang), (1, reps)), jnp.tile(jnp.sin(ang), (1, reps))


def _rot_cols(w):
    half = QK_ROPE_DIM // 2
    return jnp.concatenate([-w[..., half:], w[..., :half]], axis=-1)


def _prep_layer(l, p):
    ql, kl = p["q_a_norm"].shape[1], p["kv_a_norm"].shape[1]
    fw = p["w_fourier"].shape[1]
    w_in = p["w_in"][l]
    s0, s1, s2, s3 = fw, fw + ql, fw + ql + kl, fw + ql + kl + QK_ROPE_DIM
    w_kr = w_in[:, s2:s3]
    wlat = jnp.concatenate([w_in[:, s0:s2], w_kr, _rot_cols(w_kr)], axis=1)
    w_uq = p["w_uq"][l].reshape(ql, N_HEADS, QK_NOPE_DIM + QK_ROPE_DIM)
    w_q_rope = w_uq[:, :, QK_NOPE_DIM:]
    w_ukv = p["w_ukv"][l].reshape(kl, N_HEADS, QK_NOPE_DIM + V_HEAD_DIM)
    bf = lambda a: a.astype(BF16)
    return dict(
        ffn1_norm=p["ffn1_norm"][l][None],
        ffn2_norm=p["ffn2_norm"][l][None],
        mix_norm=p["mix_norm"][l][None],
        wf_in=bf(w_in[:, :s0]),
        wlat=bf(wlat),
        w_gate=bf(w_in[:, s3:]),
        b_gate=p["b_gate"][l][None],
        qg=p["q_a_norm"][l][None],
        kvg=p["kv_a_norm"][l][None],
        wqn=bf(w_uq[:, :, :QK_NOPE_DIM].reshape(ql, N_HEADS * QK_NOPE_DIM)),
        wqr=bf(w_q_rope.reshape(ql, N_HEADS * QK_ROPE_DIM)),
        wqrr=bf(_rot_cols(w_q_rope).reshape(ql, N_HEADS * QK_ROPE_DIM)),
        wk=bf(w_ukv[:, :, :QK_NOPE_DIM].reshape(kl, N_HEADS * QK_NOPE_DIM)),
        wvt=bf(w_ukv[:, :, QK_NOPE_DIM:].reshape(kl, N_HEADS * V_HEAD_DIM).T),
        w_fourier=bf(p["w_fourier"][l]),
        w_mla_o=bf(p["w_mla_o"][l]),
        w_out=bf(p["w_out"][l]),
    )


def _trunk(x3, layers, ffn_w, final_g, cs):
    batch, seq, d = x3.shape
    x = x3.reshape(batch * seq, d)
    fe, fo = _seq_dft_matrices(seq)
    rope_cos, rope_sin = _rope_tables(seq)
    for li, w in enumerate(layers):
        x = _ffn(x, w["ffn1_norm"], *ffn_w["ffn1"], li)
        r = _fourier_in(x, w["mix_norm"], w["wf_in"], cs, batch, seq)
        q, k, vt = _latent(x, w["mix_norm"], w["wlat"], w["qg"], w["kvg"], w["wqn"], w["wqr"], w["wqrr"],
                           w["wk"], w["wvt"], rope_cos, rope_sin, batch, seq)
        gates = _gates(x, w["mix_norm"], w["w_gate"], w["b_gate"])
        mixed = _seq_dft(fe, fo, r.reshape(batch, 2, seq, r.shape[-1]))
        att = _attention(q, k, vt, batch, seq)
        x = _merge(x, mixed, att, gates, w["w_fourier"], w["w_mla_o"], w["w_out"])
        x = _ffn(x, w["ffn2_norm"], *ffn_w["ffn2"], li, final_g=final_g if li == len(layers) - 1 else None)
    return x.reshape(batch, seq, d)


def kernel(x_prompt, x_sample, ffn1_norm, ffn1_w_gate, ffn1_w_up, ffn1_w_down, mix_norm, w_in, b_gate, q_a_norm, kv_a_norm, w_uq, w_ukv, w_fourier, w_mla_o, w_out, ffn2_norm, ffn2_w_gate, ffn2_w_up, ffn2_w_down, final_norm):
    p = dict(ffn1_norm=ffn1_norm, ffn1_w_gate=ffn1_w_gate, ffn1_w_up=ffn1_w_up, ffn1_w_down=ffn1_w_down,
             mix_norm=mix_norm, w_in=w_in, b_gate=b_gate, q_a_norm=q_a_norm, kv_a_norm=kv_a_norm, w_uq=w_uq,
             w_ukv=w_ukv, w_fourier=w_fourier, w_mla_o=w_mla_o, w_out=w_out, ffn2_norm=ffn2_norm,
             ffn2_w_gate=ffn2_w_gate, ffn2_w_up=ffn2_w_up, ffn2_w_down=ffn2_w_down)
    layers = [_prep_layer(l, p) for l in range(w_in.shape[0])]
    ffn_w = dict(ffn1=tuple(_cast_bf16(w) for w in (ffn1_w_gate, ffn1_w_up, ffn1_w_down)),
                 ffn2=tuple(_cast_bf16(w) for w in (ffn2_w_gate, ffn2_w_up, ffn2_w_down)))
    gd = w_fourier.shape[1] // N_FOURIER_GROUPS
    cos_c, sin_c = _dft_tables(gd, gd ** -0.5)
    cs = jnp.concatenate([cos_c, sin_c], axis=1).astype(BF16)
    final_g = final_norm[None]
    return (_trunk(x_prompt, layers, ffn_w, final_g, cs), _trunk(x_sample, layers, ffn_w, final_g, cs))
```

```python
import functools
import math

import jax
import jax.numpy as jnp
from jax import lax
from jax.experimental import pallas as pl
from jax.experimental.pallas import tpu as pltpu

N_HEADS = 16
QK_NOPE_DIM = 128
QK_ROPE_DIM = 64
V_HEAD_DIM = 128
N_FOURIER_GROUPS = 4
N_BRANCHES = 2
ROPE_THETA = 10000.0
NORM_EPS = 1e-6

LANES = 128
QK_PAD_DIM = 2 * LANES
VT_ROWS = V_HEAD_DIM + 16
ATTN_KEY_CHUNKS = 4
ATTN_PAIRS = 16
FFN_TILE = 512
CAST_ROWS = 256
DFT_ROW_SPLIT = 64
VMEM_LIMIT_BYTES = 56 * 1024 * 1024

F32 = jnp.float32
BF16 = jnp.bfloat16
_NT = (((1,), (1,)), ((), ()))


def _params(*semantics):
    return pltpu.CompilerParams(dimension_semantics=semantics, vmem_limit_bytes=VMEM_LIMIT_BYTES)


def _resident(shape):
    zeros = (0,) * len(shape)
    return pl.BlockSpec(shape, lambda *_: zeros, pipeline_mode=pl.Buffered(1))


def _rms(x):
    return x * lax.rsqrt(jnp.mean(x * x, axis=-1, keepdims=True) + NORM_EPS)


def _dot(a, b):
    return jnp.dot(a, b, preferred_element_type=F32)


def _cast_kernel(w_ref, o_ref):
    o_ref[...] = w_ref[...].astype(BF16)


def _cast_bf16(w, *, tr=CAST_ROWS):
    lead, r, c = w.shape
    rows = lead * r
    tr = min(tr, rows)
    assert rows % tr == 0
    out = pl.pallas_call(
        _cast_kernel,
        out_shape=jax.ShapeDtypeStruct((rows, c), BF16),
        grid=(rows // tr,),
        in_specs=[pl.BlockSpec((tr, c), lambda i: (i, 0))],
        out_specs=pl.BlockSpec((tr, c), lambda i: (i, 0)),
        compiler_params=_params("parallel"),
        name="cast_bf16",
    )(w.reshape(rows, c))
    return out.reshape(lead, r, c)


def _ffn_kernel(x_ref, g_ref, wg_ref, wu_ref, wd_ref, *rest, n_ff, n_split, final):
    if final:
        fg_ref, o_ref, h_ref = rest
    else:
        o_ref, h_ref = rest
    f = pl.program_id(1)

    def half_swiglu(h):
        a = _dot(h, wg_ref[...])
        u = _dot(h, wu_ref[...])
        act = (0.5 * (a * jax.nn.sigmoid(a)) * u).astype(BF16)
        return _dot(act, wd_ref[...])

    @pl.when(f == 0)
    def _():
        rows = x_ref.shape[0] // n_split
        for r in range(n_split):
            sl = slice(r * rows, (r + 1) * rows)
            x = x_ref[sl, :]
            h = (_rms(x) * g_ref[...]).astype(BF16)
            h_ref[sl, :] = h
            o_ref[sl, :] = x + half_swiglu(h)

    @pl.when(f > 0)
    def _():
        o_ref[...] += half_swiglu(h_ref[...])

    if final:
        @pl.when(f == n_ff - 1)
        def _():
            o_ref[...] = _rms(o_ref[...]) * fg_ref[...]


def _ffn(x, g, wg, wu, wd, layer, final_g=None, *, tm=512, tf=FFN_TILE):
    t, d = x.shape
    d_ff = wg.shape[2]
    tm, tf = min(tm, t), min(tf, d_ff)
    n_ff = d_ff // tf
    assert t % tm == 0 and d_ff % tf == 0
    final = final_g is not None
    in_specs = [
        pl.BlockSpec((tm, d), lambda i, f: (i, 0)),
        _resident((1, d)),
        pl.BlockSpec((None, d, tf), lambda i, f: (layer, 0, f)),
        pl.BlockSpec((None, d, tf), lambda i, f: (layer, 0, f)),
        pl.BlockSpec((None, tf, d), lambda i, f: (layer, f, 0)),
    ]
    args = [x, g, wg, wu, wd]
    if final:
        in_specs.append(_resident((1, d)))
        args.append(final_g)
    return pl.pallas_call(
        functools.partial(_ffn_kernel, n_ff=n_ff, n_split=2 if tm % 512 == 0 else 1, final=final),
        out_shape=jax.ShapeDtypeStruct((t, d), F32),
        grid=(t // tm, n_ff),
        in_specs=in_specs,
        out_specs=pl.BlockSpec((tm, d), lambda i, f: (i, 0)),
        scratch_shapes=[pltpu.VMEM((tm, d), BF16)],
        compiler_params=_params("parallel", "arbitrary"),
        name="ffn_final" if final else "ffn",
    )(*args)


def _fourier_in_kernel(x_ref, g_ref, wf_ref, cs_ref, perm_ref, o_ref, *, gd, n_groups, n_split):
    cs = cs_ref[...]
    rows = x_ref.shape[0] // n_split
    half = rows // 2
    for r in range(n_split):
        h = (_rms(x_ref[r * rows:(r + 1) * rows, :]) * g_ref[...]).astype(BF16)
        u = _dot(h, wf_ref[...]).astype(BF16)
        u = _dot(perm_ref[...], u).astype(BF16)
        for grp in range(n_groups):
            cols = slice(grp * gd, (grp + 1) * gd)
            a = _dot(u[:, cols], cs).astype(BF16)
            for parity in range(2):
                src = slice(parity * half, (parity + 1) * half)
                dst = slice(r * half, (r + 1) * half)
                o_ref[parity, 0, dst, cols] = a[src, :gd]
                o_ref[parity, 1, dst, cols] = a[src, gd:]


def _fourier_in(x, g, wf, cs, batch, seq, *, tm=512):
    t, d = x.shape
    fw = wf.shape[1]
    gd = fw // N_FOURIER_GROUPS
    tm = min(tm, seq)
    tps = seq // tm
    n_split = 2 if tm % 512 == 0 else 1
    rows = tm // n_split
    assert seq % tm == 0 and rows % 2 == 0
    src = jnp.arange(rows, dtype=jnp.int32)
    src = jnp.where(src < rows // 2, 2 * src, 2 * (src - rows // 2) + 1)
    perm = (src[:, None] == jnp.arange(rows, dtype=jnp.int32)[None, :]).astype(BF16)
    return pl.pallas_call(
        functools.partial(_fourier_in_kernel, gd=gd, n_groups=N_FOURIER_GROUPS, n_split=n_split),
        out_shape=jax.ShapeDtypeStruct((batch, 2, 2, seq // 2, fw), BF16),
        grid=(t // tm,),
        in_specs=[
            pl.BlockSpec((tm, d), lambda i: (i, 0)),
            _resident((1, d)),
            _resident((d, fw)),
            _resident((gd, 2 * gd)),
            _resident((rows, rows)),
        ],
        out_specs=pl.BlockSpec((None, 2, 2, tm // 2, fw), lambda i: (i // tps, 0, 0, i % tps, 0)),
        compiler_params=_params("parallel"),
        name="fourier_in",
    )(x, g, wf, cs, perm)


def _seq_dft_kernel(fe_ref, fo_ref, re_ref, ro_ref, o_ref):
    e = _dot(fe_ref[...], re_ref[...])
    o = _dot(fo_ref[...], ro_ref[...])
    o_ref[0] = (e + o).astype(BF16)
    o_ref[1] = (e - o).astype(BF16)


def _seq_dft(fe, fo, r, *, tm=256, tn=1024):
    batch, _, seq, fw = r.shape
    half = seq // 2
    tm, tn = min(tm, half), min(tn, fw)
    assert half % tm == 0 and fw % tn == 0
    out = pl.pallas_call(
        _seq_dft_kernel,
        out_shape=jax.ShapeDtypeStruct((batch, 2, half, fw), BF16),
        grid=(batch, fw // tn, half // tm),
        in_specs=[
            pl.BlockSpec((tm, seq), lambda b, n, i: (i, 0)),
            pl.BlockSpec((tm, seq), lambda b, n, i: (i, 0)),
            pl.BlockSpec((None, None, seq, tn), lambda b, n, i: (b, 0, 0, n)),
            pl.BlockSpec((None, None, seq, tn), lambda b, n, i: (b, 1, 0, n)),
        ],
        out_specs=pl.BlockSpec((None, 2, tm, tn), lambda b, n, i: (b, 0, i, n)),
        compiler_params=_params("parallel", "parallel", "arbitrary"),
        name="seq_dft",
    )(fe, fo, r, r)
    return out.reshape(batch * seq, fw)


def _latent_kernel(x_ref, g_ref, wlat_ref, qg_ref, kvg_ref, wqn_ref, wqr_ref, wqrr_ref, wk_ref, wvt_ref,
                   cos_ref, sin_ref, q_ref, k_ref, vt_ref, *, ql, kl, n_heads, dv, scale, n_split):
    rope = QK_ROPE_DIM
    rows = x_ref.shape[0] // n_split
    low = lax.broadcasted_iota(jnp.int32, (rows, LANES), 1) < rope

    def low_half(x):
        return jnp.where(low, x, 0.0).astype(BF16)

    ones = jnp.ones((VT_ROWS - dv, rows), BF16)
    for r in range(n_split):
        sl = slice(r * rows, (r + 1) * rows)
        h = (_rms(x_ref[sl, :]) * g_ref[...]).astype(BF16)
        lat = _dot(h, wlat_ref[...])
        cq = (_rms(lat[:, :ql]) * qg_ref[...]).astype(BF16)
        ckv = (_rms(lat[:, ql:ql + kl]) * kvg_ref[...]).astype(BF16)
        cos = cos_ref[sl, :]
        sin = sin_ref[sl, :]
        kk = lat[:, ql + kl:ql + kl + LANES] * jnp.where(low, cos, sin)
        k_rope = low_half(kk + pltpu.roll(kk, rope, 1))
        qn = _dot(cq, wqn_ref[...]) * scale
        qa = _dot(cq, wqr_ref[...])
        qb = _dot(cq, wqrr_ref[...])
        kn = _dot(ckv, wk_ref[...]).astype(BF16)
        vt = lax.dot_general(wvt_ref[...], ckv, _NT, preferred_element_type=F32).astype(BF16)
        for hd in range(n_heads):
            src = slice(hd * LANES, (hd + 1) * LANES)
            lo = slice(hd * QK_PAD_DIM, hd * QK_PAD_DIM + LANES)
            hi = slice(hd * QK_PAD_DIM + LANES, (hd + 1) * QK_PAD_DIM)
            pair = slice(hd // 2 * LANES, (hd // 2 + 1) * LANES)
            q_rope = (qa[:, pair] * cos + qb[:, pair] * sin) * scale
            q_ref[sl, lo] = qn[:, src].astype(BF16)
            q_ref[sl, hi] = low_half(q_rope if hd % 2 == 0 else pltpu.roll(q_rope, rope, 1))
            k_ref[sl, lo] = kn[:, src]
            k_ref[sl, hi] = k_rope
            vt_ref[hd, :dv, sl] = vt[hd * dv:(hd + 1) * dv, :]
            vt_ref[hd, dv:, sl] = ones


def _latent(x, g, wlat, qg, kvg, wqn, wqr, wqrr, wk, wvt, cos, sin, batch, seq, *, tm=256):
    t, d = x.shape
    ql, kl = qg.shape[1], kvg.shape[1]
    hn = wqn.shape[1]
    n_heads = hn // LANES
    assert n_heads % 2 == 0 and 2 * QK_ROPE_DIM == LANES
    dv = wvt.shape[0] // n_heads
    tm = min(tm, seq)
    tps = seq // tm
    assert seq % tm == 0
    scale = float(QK_NOPE_DIM + QK_ROPE_DIM) ** -0.5 * math.log2(math.e)
    row = lambda i: (i, 0)
    return pl.pallas_call(
        functools.partial(_latent_kernel, ql=ql, kl=kl, n_heads=n_heads, dv=dv, scale=scale,
                          n_split=2 if tm % 512 == 0 else 1),
        out_shape=(
            jax.ShapeDtypeStruct((t, n_heads * QK_PAD_DIM), BF16),
            jax.ShapeDtypeStruct((t, n_heads * QK_PAD_DIM), BF16),
            jax.ShapeDtypeStruct((batch, n_heads, VT_ROWS, seq), BF16),
        ),
        grid=(t // tm,),
        in_specs=[
            pl.BlockSpec((tm, d), row),
            _resident((1, d)),
            _resident(wlat.shape),
            _resident((1, ql)),
            _resident((1, kl)),
            _resident(wqn.shape),
            _resident(wqr.shape),
            _resident(wqrr.shape),
            _resident(wk.shape),
            _resident(wvt.shape),
            pl.BlockSpec((tm, LANES), lambda i: (i % tps, 0)),
            pl.BlockSpec((tm, LANES), lambda i: (i % tps, 0)),
        ],
        out_specs=(
            pl.BlockSpec((tm, n_heads * QK_PAD_DIM), row),
            pl.BlockSpec((tm, n_heads * QK_PAD_DIM), row),
            pl.BlockSpec((None, n_heads, VT_ROWS, tm), lambda i: (i // tps, 0, 0, i % tps)),
        ),
        compiler_params=_params("parallel"),
        name="latent",
    )(x, g, wlat, qg, kvg, wqn, wqr, wqrr, wk, wvt, cos, sin)


def _attn_kernel(q_ref, k_ref, vt_ref, o_ref, st_ref, pt_ref, *, n_heads, n_sub, tsub, kc, dv):
    n_chunks = k_ref.shape[0] // kc
    pairs = [(h, j) for h in range(n_heads) for j in range(n_sub)]
    col_max = [None] * len(pairs)
    for t in range(-1, len(pairs) + 1):
        run_max = None
        ot = None
        for c in range(n_chunks):
            rows = slice(c * kc, (c + 1) * kc)
            if t + 1 < len(pairs):
                h, j = pairs[t + 1]
                head = slice(h * QK_PAD_DIM, (h + 1) * QK_PAD_DIM)
                s = lax.dot_general(k_ref[rows, head], q_ref[j * tsub:(j + 1) * tsub, head], _NT,
                                    preferred_element_type=F32)
                st_ref[(t + 1) % 2, rows, :] = s
                chunk_max = jnp.max(s, axis=0, keepdims=True)
                run_max = chunk_max if run_max is None else jnp.maximum(run_max, chunk_max)
            if 0 <= t < len(pairs):
                pt_ref[t % 2, rows, :] = jnp.exp2(st_ref[t % 2, rows, :] - col_max[t]).astype(BF16)
            if t >= 1:
                h, j = pairs[t - 1]
                part = _dot(vt_ref[h, :, rows], pt_ref[(t - 1) % 2, rows, :])
                ot = part if ot is None else ot + part
        if t + 1 < len(pairs):
            col_max[t + 1] = run_max
        if t >= 1:
            h, j = pairs[t - 1]
            o = ot[:dv] / ot[dv:dv + 1]
            o_ref[j * tsub:(j + 1) * tsub, h * dv:(h + 1) * dv] = o.T.astype(BF16)


def _attention(q, k, vt, batch, seq, *, tq=4096, tsub=512, kc=None):
    t = q.shape[0]
    n_heads = q.shape[1] // QK_PAD_DIM
    dv = V_HEAD_DIM
    tq = min(tq, seq)
    tsub = min(tsub, tq)
    n_sub = tq // tsub
    hb = max(1, ATTN_PAIRS // n_sub)
    hb = hb if n_heads % hb == 0 else 1
    kc = seq // ATTN_KEY_CHUNKS if kc is None else min(kc, seq)
    tps = seq // tq
    assert seq % tq == 0 and tq % tsub == 0 and seq % kc == 0
    return pl.pallas_call(
        functools.partial(_attn_kernel, n_heads=hb, n_sub=n_sub, tsub=tsub, kc=kc, dv=dv),
        out_shape=jax.ShapeDtypeStruct((t, n_heads * dv), BF16),
        grid=(batch, n_heads // hb, tps),
        in_specs=[
            pl.BlockSpec((tq, hb * QK_PAD_DIM), lambda b, h, i: (b * tps + i, h)),
            pl.BlockSpec((seq, hb * QK_PAD_DIM), lambda b, h, i: (b, h)),
            pl.BlockSpec((None, hb, VT_ROWS, seq), lambda b, h, i: (b, h, 0, 0)),
        ],
        out_specs=pl.BlockSpec((tq, hb * dv), lambda b, h, i: (b * tps + i, h)),
        scratch_shapes=[pltpu.VMEM((2, seq, tsub), F32), pltpu.VMEM((2, seq, tsub), BF16)],
        compiler_params=_params("parallel", "parallel", "arbitrary"),
        name="attention",
    )(q, k, vt)


def _gate_kernel(x_ref, g_ref, w_ref, b_ref, o_ref, *, n_split):
    rows = x_ref.shape[0] // n_split
    for r in range(n_split):
        sl = slice(r * rows, (r + 1) * rows)
        h = (_rms(x_ref[sl, :]) * g_ref[...]).astype(BF16)
        o_ref[sl, :] = jax.nn.sigmoid(_dot(h, w_ref[...]) + b_ref[...]).astype(BF16)


def _gates(x, g, w, b, *, tm=512):
    t, d = x.shape
    n = w.shape[1]
    tm = min(tm, t)
    assert t % tm == 0
    return pl.pallas_call(
        functools.partial(_gate_kernel, n_split=2 if tm % 512 == 0 else 1),
        out_shape=jax.ShapeDtypeStruct((t, n), BF16),
        grid=(t // tm,),
        in_specs=[
            pl.BlockSpec((tm, d), lambda i: (i, 0)),
            _resident((1, d)),
            _resident((d, n)),
            _resident((1, n)),
        ],
        out_specs=pl.BlockSpec((tm, n), lambda i: (i, 0)),
        compiler_params=_params("parallel"),
        name="gates",
    )(x, g, w, b)


def _merge_kernel(x_ref, mix_ref, att_ref, gate_ref, wf_ref, wo_ref, wout_ref, o_ref, *, d):
    ya = _dot(mix_ref[...], wf_ref[...])
    yb = _dot(att_ref[...], wo_ref[...])
    m = gate_ref[:, :d].astype(F32) * ya + gate_ref[:, d:].astype(F32) * yb
    o_ref[...] = x_ref[...] + _dot(m.astype(BF16), wout_ref[...])


def _merge(x, mixed, att, gates, wf, wo, wout, *, tm=256):
    t, d = x.shape
    tm = min(tm, t)
    assert t % tm == 0
    row = lambda i: (i, 0)
    return pl.pallas_call(
        functools.partial(_merge_kernel, d=d),
        out_shape=jax.ShapeDtypeStruct((t, d), F32),
        grid=(t // tm,),
        in_specs=[
            pl.BlockSpec((tm, d), row),
            pl.BlockSpec((tm, mixed.shape[1]), row),
            pl.BlockSpec((tm, att.shape[1]), row),
            pl.BlockSpec((tm, gates.shape[1]), row),
            _resident(wf.shape),
            _resident(wo.shape),
            _resident(wout.shape),
        ],
        out_specs=pl.BlockSpec((tm, d), row),
        compiler_params=_params("parallel"),
        name="merge",
    )(x, mixed, att, gates, wf, wo, wout)


def _dft_tables(n, scale):
    def trig(rows, k):
        ang = ((rows[:, None] * k[None, :]) % n).astype(F32) * (2.0 * math.pi / n)
        return jnp.cos(ang), jnp.sin(ang)

    k = jnp.arange(n, dtype=jnp.int32)
    m = DFT_ROW_SPLIT
    if n % m or n <= m:
        c, s = trig(k, k)
        return c * scale, s * scale
    ca, sa = trig(jnp.arange(n // m, dtype=jnp.int32) * m, k)
    cb, sb = trig(jnp.arange(m, dtype=jnp.int32), k)
    cb, sb = cb * scale, sb * scale
    c = ca[:, None, :] * cb[None, :, :] - sa[:, None, :] * sb[None, :, :]
    s = sa[:, None, :] * cb[None, :, :] + ca[:, None, :] * sb[None, :, :]
    return c.reshape(n, n), s.reshape(n, n)


def _seq_dft_matrices(seq):
    half = seq // 2
    c, s = _dft_tables(half, seq ** -0.5)
    phi = jnp.arange(half, dtype=F32)[:, None] * (2.0 * math.pi / seq)
    co, so = c * jnp.cos(phi) - s * jnp.sin(phi), s * jnp.cos(phi) + c * jnp.sin(phi)
    fe = jnp.concatenate([c, -s], axis=1).astype(BF16)
    fo = jnp.concatenate([co, -so], axis=1).astype(BF16)
    return fe, fo


def _rope_tables(seq):
    half = QK_ROPE_DIM // 2
    inv_freq = 1.0 / (ROPE_THETA ** (jnp.arange(half, dtype=F32) / half))
    ang = jnp.arange(seq, dtype=F32)[:, None] * inv_freq[None, :]
    reps = LANES // half
    return jnp.tile(jnp.cos(ang), (1, reps)), jnp.tile(jnp.sin(ang), (1, reps))


def _rot_cols(w):
    half = QK_ROPE_DIM // 2
    return jnp.concatenate([-w[..., half:], w[..., :half]], axis=-1)


def _prep_layer(l, p):
    ql, kl = p["q_a_norm"].shape[1], p["kv_a_norm"].shape[1]
    fw = p["w_fourier"].shape[1]
    w_in = p["w_in"][l]
    s0, s1, s2, s3 = fw, fw + ql, fw + ql + kl, fw + ql + kl + QK_ROPE_DIM
    w_kr = w_in[:, s2:s3]
    wlat = jnp.concatenate([w_in[:, s0:s2], w_kr, _rot_cols(w_kr)], axis=1)
    w_uq = p["w_uq"][l].reshape(ql, N_HEADS, QK_NOPE_DIM + QK_ROPE_DIM)
    w_q_rope = w_uq[:, :, QK_NOPE_DIM:]
    w_ukv = p["w_ukv"][l].reshape(kl, N_HEADS, QK_NOPE_DIM + V_HEAD_DIM)
    bf = lambda a: a.astype(BF16)
    return dict(
        ffn1_norm=p["ffn1_norm"][l][None],
        ffn2_norm=p["ffn2_norm"][l][None],
        mix_norm=p["mix_norm"][l][None],
        wf_in=bf(w_in[:, :s0]),
        wlat=bf(wlat),
        w_gate=bf(w_in[:, s3:]),
        b_gate=p["b_gate"][l][None],
        qg=p["q_a_norm"][l][None],
        kvg=p["kv_a_norm"][l][None],
        wqn=bf(w_uq[:, :, :QK_NOPE_DIM].reshape(ql, N_HEADS * QK_NOPE_DIM)),
        wqr=bf(w_q_rope.reshape(ql, N_HEADS * QK_ROPE_DIM)),
        wqrr=bf(_rot_cols(w_q_rope).reshape(ql, N_HEADS * QK_ROPE_DIM)),
        wk=bf(w_ukv[:, :, :QK_NOPE_DIM].reshape(kl, N_HEADS * QK_NOPE_DIM)),
        wvt=bf(w_ukv[:, :, QK_NOPE_DIM:].reshape(kl, N_HEADS * V_HEAD_DIM).T),
        w_fourier=bf(p["w_fourier"][l]),
        w_mla_o=bf(p["w_mla_o"][l]),
        w_out=bf(p["w_out"][l]),
    )


def _trunk(x3, layers, ffn_w, final_g, cs):
    batch, seq, d = x3.shape
    x = x3.reshape(batch * seq, d)
    fe, fo = _seq_dft_matrices(seq)
    rope_cos, rope_sin = _rope_tables(seq)
    for li, w in enumerate(layers):
        x = _ffn(x, w["ffn1_norm"], *ffn_w["ffn1"], li)
        r = _fourier_in(x, w["mix_norm"], w["wf_in"], cs, batch, seq)
        q, k, vt = _latent(x, w["mix_norm"], w["wlat"], w["qg"], w["kvg"], w["wqn"], w["wqr"], w["wqrr"],
                           w["wk"], w["wvt"], rope_cos, rope_sin, batch, seq)
        gates = _gates(x, w["mix_norm"], w["w_gate"], w["b_gate"])
        mixed = _seq_dft(fe, fo, r.reshape(batch, 2, seq, r.shape[-1]))
        att = _attention(q, k, vt, batch, seq)
        x = _merge(x, mixed, att, gates, w["w_fourier"], w["w_mla_o"], w["w_out"])
        x = _ffn(x, w["ffn2_norm"], *ffn_w["ffn2"], li, final_g=final_g if li == len(layers) - 1 else None)
    return x.reshape(batch, seq, d)


def kernel(x_prompt, x_sample, ffn1_norm, ffn1_w_gate, ffn1_w_up, ffn1_w_down, mix_norm, w_in, b_gate, q_a_norm, kv_a_norm, w_uq, w_ukv, w_fourier, w_mla_o, w_out, ffn2_norm, ffn2_w_gate, ffn2_w_up, ffn2_w_down, final_norm):
    p = dict(ffn1_norm=ffn1_norm, ffn1_w_gate=ffn1_w_gate, ffn1_w_up=ffn1_w_up, ffn1_w_down=ffn1_w_down,
             mix_norm=mix_norm, w_in=w_in, b_gate=b_gate, q_a_norm=q_a_norm, kv_a_norm=kv_a_norm, w_uq=w_uq,
             w_ukv=w_ukv, w_fourier=w_fourier, w_mla_o=w_mla_o, w_out=w_out, ffn2_norm=ffn2_norm,
             ffn2_w_gate=ffn2_w_gate, ffn2_w_up=ffn2_w_up, ffn2_w_down=ffn2_w_down)
    layers = [_prep_layer(l, p) for l in range(w_in.shape[0])]
    ffn_w = dict(ffn1=tuple(_cast_bf16(w) for w in (ffn1_w_gate, ffn1_w_up, ffn1_w_down)),
                 ffn2=tuple(_cast_bf16(w) for w in (ffn2_w_gate, ffn2_w_up, ffn2_w_down)))
    gd = w_fourier.shape[1] // N_FOURIER_GROUPS
    cos_c, sin_c = _dft_tables(gd, gd ** -0.5)
    cs = jnp.concatenate([cos_c, sin_c], axis=1).astype(BF16)
    final_g = final_norm[None]
    return (_trunk(x_prompt, layers, ffn_w, final_g, cs), _trunk(x_sample, layers, ffn_w, final_g, cs))
```

```python
import functools
import math

import jax
import jax.numpy as jnp
from jax import lax
from jax.experimental import pallas as pl
from jax.experimental.pallas import tpu as pltpu

N_HEADS = 16
QK_NOPE_DIM = 128
QK_ROPE_DIM = 64
V_HEAD_DIM = 128
N_FOURIER_GROUPS = 4
N_BRANCHES = 2
ROPE_THETA = 10000.0
NORM_EPS = 1e-6

LANES = 128
QK_PAD_DIM = 2 * LANES
VT_ROWS = V_HEAD_DIM + 16
ATTN_KEY_CHUNKS = 4
ATTN_PAIRS = 16
FFN_TILE = 512
CAST_ROWS = 256
DFT_ROW_SPLIT = 64
VMEM_LIMIT_BYTES = 56 * 1024 * 1024

F32 = jnp.float32
BF16 = jnp.bfloat16
_NT = (((1,), (1,)), ((), ()))


def _params(*semantics):
    return pltpu.CompilerParams(dimension_semantics=semantics, vmem_limit_bytes=VMEM_LIMIT_BYTES)


def _resident(shape):
    zeros = (0,) * len(shape)
    return pl.BlockSpec(shape, lambda *_: zeros, pipeline_mode=pl.Buffered(1))


def _rms(x):
    return x * lax.rsqrt(jnp.mean(x * x, axis=-1, keepdims=True) + NORM_EPS)


def _dot(a, b):
    return jnp.dot(a, b, preferred_element_type=F32)


def _cast_kernel(w_ref, o_ref):
    o_ref[...] = w_ref[...].astype(BF16)


def _cast_bf16(w, *, tr=CAST_ROWS):
    lead, r, c = w.shape
    rows = lead * r
    tr = min(tr, rows)
    assert rows % tr == 0
    out = pl.pallas_call(
        _cast_kernel,
        out_shape=jax.ShapeDtypeStruct((rows, c), BF16),
        grid=(rows // tr,),
        in_specs=[pl.BlockSpec((tr, c), lambda i: (i, 0))],
        out_specs=pl.BlockSpec((tr, c), lambda i: (i, 0)),
        compiler_params=_params("parallel"),
        name="cast_bf16",
    )(w.reshape(rows, c))
    return out.reshape(lead, r, c)


def _ffn_kernel(x_ref, g_ref, wg_ref, wu_ref, wd_ref, *rest, n_ff, n_split, final):
    if final:
        fg_ref, o_ref, h_ref = rest
    else:
        o_ref, h_ref = rest
    f = pl.program_id(1)

    def half_swiglu(h):
        a = _dot(h, wg_ref[...])
        u = _dot(h, wu_ref[...])
        act = (0.5 * (a * jax.nn.sigmoid(a)) * u).astype(BF16)
        return _dot(act, wd_ref[...])

    @pl.when(f == 0)
    def _():
        rows = x_ref.shape[0] // n_split
        for r in range(n_split):
            sl = slice(r * rows, (r + 1) * rows)
            x = x_ref[sl, :]
            h = (_rms(x) * g_ref[...]).astype(BF16)
            h_ref[sl, :] = h
            o_ref[sl, :] = x + half_swiglu(h)

    @pl.when(f > 0)
    def _():
        o_ref[...] += half_swiglu(h_ref[...])

    if final:
        @pl.when(f == n_ff - 1)
        def _():
            o_ref[...] = _rms(o_ref[...]) * fg_ref[...]


def _ffn(x, g, wg, wu, wd, layer, final_g=None, *, tm=1024, tf=FFN_TILE):
    t, d = x.shape
    d_ff = wg.shape[2]
    tm, tf = min(tm, t), min(tf, d_ff)
    n_ff = d_ff // tf
    assert t % tm == 0 and d_ff % tf == 0
    final = final_g is not None
    in_specs = [
        pl.BlockSpec((tm, d), lambda i, f: (i, 0)),
        _resident((1, d)),
        pl.BlockSpec((None, d, tf), lambda i, f: (layer, 0, f)),
        pl.BlockSpec((None, d, tf), lambda i, f: (layer, 0, f)),
        pl.BlockSpec((None, tf, d), lambda i, f: (layer, f, 0)),
    ]
    args = [x, g, wg, wu, wd]
    if final:
        in_specs.append(_resident((1, d)))
        args.append(final_g)
    return pl.pallas_call(
        functools.partial(_ffn_kernel, n_ff=n_ff, n_split=2 if tm % 512 == 0 else 1, final=final),
        out_shape=jax.ShapeDtypeStruct((t, d), F32),
        grid=(t // tm, n_ff),
        in_specs=in_specs,
        out_specs=pl.BlockSpec((tm, d), lambda i, f: (i, 0)),
        scratch_shapes=[pltpu.VMEM((tm, d), BF16)],
        compiler_params=_params("parallel", "arbitrary"),
        name="ffn_final" if final else "ffn",
    )(*args)


def _fourier_in_kernel(x_ref, g_ref, wf_ref, cs_ref, perm_ref, o_ref, *, gd, n_groups, n_split):
    cs = cs_ref[...]
    rows = x_ref.shape[0] // n_split
    half = rows // 2
    for r in range(n_split):
        h = (_rms(x_ref[r * rows:(r + 1) * rows, :]) * g_ref[...]).astype(BF16)
        u = _dot(h, wf_ref[...]).astype(BF16)
        u = _dot(perm_ref[...], u).astype(BF16)
        for grp in range(n_groups):
            cols = slice(grp * gd, (grp + 1) * gd)
            a = _dot(u[:, cols], cs).astype(BF16)
            for parity in range(2):
                src = slice(parity * half, (parity + 1) * half)
                dst = slice(r * half, (r + 1) * half)
                o_ref[parity, 0, dst, cols] = a[src, :gd]
                o_ref[parity, 1, dst, cols] = a[src, gd:]


def _fourier_in(x, g, wf, cs, batch, seq, *, tm=512):
    t, d = x.shape
    fw = wf.shape[1]
    gd = fw // N_FOURIER_GROUPS
    tm = min(tm, seq)
    tps = seq // tm
    n_split = 2 if tm % 512 == 0 else 1
    rows = tm // n_split
    assert seq % tm == 0 and rows % 2 == 0
    src = jnp.arange(rows, dtype=jnp.int32)
    src = jnp.where(src < rows // 2, 2 * src, 2 * (src - rows // 2) + 1)
    perm = (src[:, None] == jnp.arange(rows, dtype=jnp.int32)[None, :]).astype(BF16)
    return pl.pallas_call(
        functools.partial(_fourier_in_kernel, gd=gd, n_groups=N_FOURIER_GROUPS, n_split=n_split),
        out_shape=jax.ShapeDtypeStruct((batch, 2, 2, seq // 2, fw), BF16),
        grid=(t // tm,),
        in_specs=[
            pl.BlockSpec((tm, d), lambda i: (i, 0)),
            _resident((1, d)),
            _resident((d, fw)),
            _resident((gd, 2 * gd)),
            _resident((rows, rows)),
        ],
        out_specs=pl.BlockSpec((None, 2, 2, tm // 2, fw), lambda i: (i // tps, 0, 0, i % tps, 0)),
        compiler_params=_params("parallel"),
        name="fourier_in",
    )(x, g, wf, cs, perm)


def _seq_dft_kernel(fe_ref, fo_ref, re_ref, ro_ref, o_ref):
    e = _dot(fe_ref[...], re_ref[...])
    o = _dot(fo_ref[...], ro_ref[...])
    o_ref[0] = (e + o).astype(BF16)
    o_ref[1] = (e - o).astype(BF16)


def _seq_dft(fe, fo, r, *, tm=256, tn=1024):
    batch, _, seq, fw = r.shape
    half = seq // 2
    tm, tn = min(tm, half), min(tn, fw)
    assert half % tm == 0 and fw % tn == 0
    out = pl.pallas_call(
        _seq_dft_kernel,
        out_shape=jax.ShapeDtypeStruct((batch, 2, half, fw), BF16),
        grid=(batch, fw // tn, half // tm),
        in_specs=[
            pl.BlockSpec((tm, seq), lambda b, n, i: (i, 0)),
            pl.BlockSpec((tm, seq), lambda b, n, i: (i, 0)),
            pl.BlockSpec((None, None, seq, tn), lambda b, n, i: (b, 0, 0, n)),
            pl.BlockSpec((None, None, seq, tn), lambda b, n, i: (b, 1, 0, n)),
        ],
        out_specs=pl.BlockSpec((None, 2, tm, tn), lambda b, n, i: (b, 0, i, n)),
        compiler_params=_params("parallel", "parallel", "arbitrary"),
        name="seq_dft",
    )(fe, fo, r, r)
    return out.reshape(batch * seq, fw)


def _latent_kernel(x_ref, g_ref, wlat_ref, qg_ref, kvg_ref, wqn_ref, wqr_ref, wqrr_ref, wk_ref, wvt_ref,
                   cos_ref, sin_ref, q_ref, k_ref, vt_ref, *, ql, kl, n_heads, dv, scale, n_split):
    rope = QK_ROPE_DIM
    rows = x_ref.shape[0] // n_split
    low = lax.broadcasted_iota(jnp.int32, (rows, LANES), 1) < rope

    def low_half(x):
        return jnp.where(low, x, 0.0).astype(BF16)

    ones = jnp.ones((VT_ROWS - dv, rows), BF16)
    for r in range(n_split):
        sl = slice(r * rows, (r + 1) * rows)
        h = (_rms(x_ref[sl, :]) * g_ref[...]).astype(BF16)
        lat = _dot(h, wlat_ref[...])
        cq = (_rms(lat[:, :ql]) * qg_ref[...]).astype(BF16)
        ckv = (_rms(lat[:, ql:ql + kl]) * kvg_ref[...]).astype(BF16)
        cos = cos_ref[sl, :]
        sin = sin_ref[sl, :]
        kk = lat[:, ql + kl:ql + kl + LANES] * jnp.where(low, cos, sin)
        k_rope = low_half(kk + pltpu.roll(kk, rope, 1))
        qn = _dot(cq, wqn_ref[...]) * scale
        qa = _dot(cq, wqr_ref[...])
        qb = _dot(cq, wqrr_ref[...])
        kn = _dot(ckv, wk_ref[...]).astype(BF16)
        vt = lax.dot_general(wvt_ref[...], ckv, _NT, preferred_element_type=F32).astype(BF16)
        for hd in range(n_heads):
            src = slice(hd * LANES, (hd + 1) * LANES)
            lo = slice(hd * QK_PAD_DIM, hd * QK_PAD_DIM + LANES)
            hi = slice(hd * QK_PAD_DIM + LANES, (hd + 1) * QK_PAD_DIM)
            pair = slice(hd // 2 * LANES, (hd // 2 + 1) * LANES)
            q_rope = (qa[:, pair] * cos + qb[:, pair] * sin) * scale
            q_ref[sl, lo] = qn[:, src].astype(BF16)
            q_ref[sl, hi] = low_half(q_rope if hd % 2 == 0 else pltpu.roll(q_rope, rope, 1))
            k_ref[sl, lo] = kn[:, src]
            k_ref[sl, hi] = k_rope
            vt_ref[hd, :dv, sl] = vt[hd * dv:(hd + 1) * dv, :]
            vt_ref[hd, dv:, sl] = ones


def _latent(x, g, wlat, qg, kvg, wqn, wqr, wqrr, wk, wvt, cos, sin, batch, seq, *, tm=256):
    t, d = x.shape
    ql, kl = qg.shape[1], kvg.shape[1]
    hn = wqn.shape[1]
    n_heads = hn // LANES
    assert n_heads % 2 == 0 and 2 * QK_ROPE_DIM == LANES
    dv = wvt.shape[0] // n_heads
    tm = min(tm, seq)
    tps = seq // tm
    assert seq % tm == 0
    scale = float(QK_NOPE_DIM + QK_ROPE_DIM) ** -0.5 * math.log2(math.e)
    row = lambda i: (i, 0)
    return pl.pallas_call(
        functools.partial(_latent_kernel, ql=ql, kl=kl, n_heads=n_heads, dv=dv, scale=scale,
                          n_split=2 if tm % 512 == 0 else 1),
        out_shape=(
            jax.ShapeDtypeStruct((t, n_heads * QK_PAD_DIM), BF16),
            jax.ShapeDtypeStruct((t, n_heads * QK_PAD_DIM), BF16),
            jax.ShapeDtypeStruct((batch, n_heads, VT_ROWS, seq), BF16),
        ),
        grid=(t // tm,),
        in_specs=[
            pl.BlockSpec((tm, d), row),
            _resident((1, d)),
            _resident(wlat.shape),
            _resident((1, ql)),
            _resident((1, kl)),
            _resident(wqn.shape),
            _resident(wqr.shape),
            _resident(wqrr.shape),
            _resident(wk.shape),
            _resident(wvt.shape),
            pl.BlockSpec((tm, LANES), lambda i: (i % tps, 0)),
            pl.BlockSpec((tm, LANES), lambda i: (i % tps, 0)),
        ],
        out_specs=(
            pl.BlockSpec((tm, n_heads * QK_PAD_DIM), row),
            pl.BlockSpec((tm, n_heads * QK_PAD_DIM), row),
            pl.BlockSpec((None, n_heads, VT_ROWS, tm), lambda i: (i // tps, 0, 0, i % tps)),
        ),
        compiler_params=_params("parallel"),
        name="latent",
    )(x, g, wlat, qg, kvg, wqn, wqr, wqrr, wk, wvt, cos, sin)


def _attn_kernel(q_ref, k_ref, vt_ref, o_ref, st_ref, pt_ref, *, n_heads, n_sub, tsub, kc, dv):
    n_chunks = k_ref.shape[0] // kc
    pairs = [(h, j) for h in range(n_heads) for j in range(n_sub)]
    col_max = [None] * len(pairs)
    for t in range(-1, len(pairs) + 1):
        run_max = None
        ot = None
        for c in range(n_chunks):
            rows = slice(c * kc, (c + 1) * kc)
            if t + 1 < len(pairs):
                h, j = pairs[t + 1]
                head = slice(h * QK_PAD_DIM, (h + 1) * QK_PAD_DIM)
                s = lax.dot_general(k_ref[rows, head], q_ref[j * tsub:(j + 1) * tsub, head], _NT,
                                    preferred_element_type=F32)
                st_ref[(t + 1) % 2, rows, :] = s
                chunk_max = jnp.max(s, axis=0, keepdims=True)
                run_max = chunk_max if run_max is None else jnp.maximum(run_max, chunk_max)
            if 0 <= t < len(pairs):
                pt_ref[t % 2, rows, :] = jnp.exp2(st_ref[t % 2, rows, :] - col_max[t]).astype(BF16)
            if t >= 1:
                h, j = pairs[t - 1]
                part = _dot(vt_ref[h, :, rows], pt_ref[(t - 1) % 2, rows, :])
                ot = part if ot is None else ot + part
        if t + 1 < len(pairs):
            col_max[t + 1] = run_max
        if t >= 1:
            h, j = pairs[t - 1]
            o = ot[:dv] / ot[dv:dv + 1]
            o_ref[j * tsub:(j + 1) * tsub, h * dv:(h + 1) * dv] = o.T.astype(BF16)


def _attention(q, k, vt, batch, seq, *, tq=4096, tsub=512, kc=None):
    t = q.shape[0]
    n_heads = q.shape[1] // QK_PAD_DIM
    dv = V_HEAD_DIM
    tq = min(tq, seq)
    tsub = min(tsub, tq)
    n_sub = tq // tsub
    hb = max(1, ATTN_PAIRS // n_sub)
    hb = hb if n_heads % hb == 0 else 1
    kc = seq // ATTN_KEY_CHUNKS if kc is None else min(kc, seq)
    tps = seq // tq
    assert seq % tq == 0 and tq % tsub == 0 and seq % kc == 0
    return pl.pallas_call(
        functools.partial(_attn_kernel, n_heads=hb, n_sub=n_sub, tsub=tsub, kc=kc, dv=dv),
        out_shape=jax.ShapeDtypeStruct((t, n_heads * dv), BF16),
        grid=(batch, n_heads // hb, tps),
        in_specs=[
            pl.BlockSpec((tq, hb * QK_PAD_DIM), lambda b, h, i: (b * tps + i, h)),
            pl.BlockSpec((seq, hb * QK_PAD_DIM), lambda b, h, i: (b, h)),
            pl.BlockSpec((None, hb, VT_ROWS, seq), lambda b, h, i: (b, h, 0, 0)),
        ],
        out_specs=pl.BlockSpec((tq, hb * dv), lambda b, h, i: (b * tps + i, h)),
        scratch_shapes=[pltpu.VMEM((2, seq, tsub), F32), pltpu.VMEM((2, seq, tsub), BF16)],
        compiler_params=_params("parallel", "parallel", "arbitrary"),
        name="attention",
    )(q, k, vt)


def _gate_kernel(x_ref, g_ref, w_ref, b_ref, o_ref, *, n_split):
    rows = x_ref.shape[0] // n_split
    for r in range(n_split):
        sl = slice(r * rows, (r + 1) * rows)
        h = (_rms(x_ref[sl, :]) * g_ref[...]).astype(BF16)
        o_ref[sl, :] = jax.nn.sigmoid(_dot(h, w_ref[...]) + b_ref[...]).astype(BF16)


def _gates(x, g, w, b, *, tm=512):
    t, d = x.shape
    n = w.shape[1]
    tm = min(tm, t)
    assert t % tm == 0
    return pl.pallas_call(
        functools.partial(_gate_kernel, n_split=2 if tm % 512 == 0 else 1),
        out_shape=jax.ShapeDtypeStruct((t, n), BF16),
        grid=(t // tm,),
        in_specs=[
            pl.BlockSpec((tm, d), lambda i: (i, 0)),
            _resident((1, d)),
            _resident((d, n)),
            _resident((1, n)),
        ],
        out_specs=pl.BlockSpec((tm, n), lambda i: (i, 0)),
        compiler_params=_params("parallel"),
        name="gates",
    )(x, g, w, b)


def _merge_kernel(x_ref, mix_ref, att_ref, gate_ref, wf_ref, wo_ref, wout_ref, o_ref, *, d):
    ya = _dot(mix_ref[...], wf_ref[...])
    yb = _dot(att_ref[...], wo_ref[...])
    m = gate_ref[:, :d].astype(F32) * ya + gate_ref[:, d:].astype(F32) * yb
    o_ref[...] = x_ref[...] + _dot(m.astype(BF16), wout_ref[...])


def _merge(x, mixed, att, gates, wf, wo, wout, *, tm=256):
    t, d = x.shape
    tm = min(tm, t)
    assert t % tm == 0
    row = lambda i: (i, 0)
    return pl.pallas_call(
        functools.partial(_merge_kernel, d=d),
        out_shape=jax.ShapeDtypeStruct((t, d), F32),
        grid=(t // tm,),
        in_specs=[
            pl.BlockSpec((tm, d), row),
            pl.BlockSpec((tm, mixed.shape[1]), row),
            pl.BlockSpec((tm, att.shape[1]), row),
            pl.BlockSpec((tm, gates.shape[1]), row),
            _resident(wf.shape),
            _resident(wo.shape),
            _resident(wout.shape),
        ],
        out_specs=pl.BlockSpec((tm, d), row),
        compiler_params=_params("parallel"),
        name="merge",
    )(x, mixed, att, gates, wf, wo, wout)


def _dft_tables(n, scale):
    def trig(rows, k):
        ang = ((rows[:, None] * k[None, :]) % n).astype(F32) * (2.0 * math.pi / n)
        return jnp.cos(ang), jnp.sin(ang)

    k = jnp.arange(n, dtype=jnp.int32)
    m = DFT_ROW_SPLIT
    if n % m or n <= m:
        c, s = trig(k, k)
        return c * scale, s * scale
    ca, sa = trig(jnp.arange(n // m, dtype=jnp.int32) * m, k)
    cb, sb = trig(jnp.arange(m, dtype=jnp.int32), k)
    cb, sb = cb * scale, sb * scale
    c = ca[:, None, :] * cb[None, :, :] - sa[:, None, :] * sb[None, :, :]
    s = sa[:, None, :] * cb[None, :, :] + ca[:, None, :] * sb[None, :, :]
    return c.reshape(n, n), s.reshape(n, n)


def _seq_dft_matrices(seq):
    half = seq // 2
    c, s = _dft_tables(half, seq ** -0.5)
    phi = jnp.arange(half, dtype=F32)[:, None] * (2.0 * math.pi / seq)
    co, so = c * jnp.cos(phi) - s * jnp.sin(phi), s * jnp.cos(phi) + c * jnp.sin(phi)
    fe = jnp.concatenate([c, -s], axis=1).astype(BF16)
    fo = jnp.concatenate([co, -so], axis=1).astype(BF16)
    return fe, fo


def _rope_tables(seq):
    half = QK_ROPE_DIM // 2
    inv_freq = 1.0 / (ROPE_THETA ** (jnp.arange(half, dtype=F32) / half))
    ang = jnp.arange(seq, dtype=F32)[:, None] * inv_freq[None, :]
    reps = LANES // half
    return jnp.tile(jnp.cos(ang), (1, reps)), jnp.tile(jnp.sin(ang), (1, reps))


def _rot_cols(w):
    half = QK_ROPE_DIM // 2
    return jnp.concatenate([-w[..., half:], w[..., :half]], axis=-1)


def _prep_layer(l, p):
    ql, kl = p["q_a_norm"].shape[1], p["kv_a_norm"].shape[1]
    fw = p["w_fourier"].shape[1]
    w_in = p["w_in"][l]
    s0, s1, s2, s3 = fw, fw + ql, fw + ql + kl, fw + ql + kl + QK_ROPE_DIM
    w_kr = w_in[:, s2:s3]
    wlat = jnp.concatenate([w_in[:, s0:s2], w_kr, _rot_cols(w_kr)], axis=1)
    w_uq = p["w_uq"][l].reshape(ql, N_HEADS, QK_NOPE_DIM + QK_ROPE_DIM)
    w_q_rope = w_uq[:, :, QK_NOPE_DIM:]
    w_ukv = p["w_ukv"][l].reshape(kl, N_HEADS, QK_NOPE_DIM + V_HEAD_DIM)
    bf = lambda a: a.astype(BF16)
    return dict(
        ffn1_norm=p["ffn1_norm"][l][None],
        ffn2_norm=p["ffn2_norm"][l][None],
        mix_norm=p["mix_norm"][l][None],
        wf_in=bf(w_in[:, :s0]),
        wlat=bf(wlat),
        w_gate=bf(w_in[:, s3:]),
        b_gate=p["b_gate"][l][None],
        qg=p["q_a_norm"][l][None],
        kvg=p["kv_a_norm"][l][None],
        wqn=bf(w_uq[:, :, :QK_NOPE_DIM].reshape(ql, N_HEADS * QK_NOPE_DIM)),
        wqr=bf(w_q_rope.reshape(ql, N_HEADS * QK_ROPE_DIM)),
        wqrr=bf(_rot_cols(w_q_rope).reshape(ql, N_HEADS * QK_ROPE_DIM)),
        wk=bf(w_ukv[:, :, :QK_NOPE_DIM].reshape(kl, N_HEADS * QK_NOPE_DIM)),
        wvt=bf(w_ukv[:, :, QK_NOPE_DIM:].reshape(kl, N_HEADS * V_HEAD_DIM).T),
        w_fourier=bf(p["w_fourier"][l]),
        w_mla_o=bf(p["w_mla_o"][l]),
        w_out=bf(p["w_out"][l]),
    )


def _trunk(x3, layers, ffn_w, final_g, cs):
    batch, seq, d = x3.shape
    x = x3.reshape(batch * seq, d)
    fe, fo = _seq_dft_matrices(seq)
    rope_cos, rope_sin = _rope_tables(seq)
    for li, w in enumerate(layers):
        x = _ffn(x, w["ffn1_norm"], *ffn_w["ffn1"], li)
        r = _fourier_in(x, w["mix_norm"], w["wf_in"], cs, batch, seq)
        q, k, vt = _latent(x, w["mix_norm"], w["wlat"], w["qg"], w["kvg"], w["wqn"], w["wqr"], w["wqrr"],
                           w["wk"], w["wvt"], rope_cos, rope_sin, batch, seq)
        gates = _gates(x, w["mix_norm"], w["w_gate"], w["b_gate"])
        mixed = _seq_dft(fe, fo, r.reshape(batch, 2, seq, r.shape[-1]))
        att = _attention(q, k, vt, batch, seq)
        x = _merge(x, mixed, att, gates, w["w_fourier"], w["w_mla_o"], w["w_out"])
        x = _ffn(x, w["ffn2_norm"], *ffn_w["ffn2"], li, final_g=final_g if li == len(layers) - 1 else None)
    return x.reshape(batch, seq, d)


def kernel(x_prompt, x_sample, ffn1_norm, ffn1_w_gate, ffn1_w_up, ffn1_w_down, mix_norm, w_in, b_gate, q_a_norm, kv_a_norm, w_uq, w_ukv, w_fourier, w_mla_o, w_out, ffn2_norm, ffn2_w_gate, ffn2_w_up, ffn2_w_down, final_norm):
    p = dict(ffn1_norm=ffn1_norm, ffn1_w_gate=ffn1_w_gate, ffn1_w_up=ffn1_w_up, ffn1_w_down=ffn1_w_down,
             mix_norm=mix_norm, w_in=w_in, b_gate=b_gate, q_a_norm=q_a_norm, kv_a_norm=kv_a_norm, w_uq=w_uq,
             w_ukv=w_ukv, w_fourier=w_fourier, w_mla_o=w_mla_o, w_out=w_out, ffn2_norm=ffn2_norm,
             ffn2_w_gate=ffn2_w_gate, ffn2_w_up=ffn2_w_up, ffn2_w_down=ffn2_w_down)
    layers = [_prep_layer(l, p) for l in range(w_in.shape[0])]
    ffn_w = dict(ffn1=tuple(_cast_bf16(w) for w in (ffn1_w_gate, ffn1_w_up, ffn1_w_down)),
                 ffn2=tuple(_cast_bf16(w) for w in (ffn2_w_gate, ffn2_w_up, ffn2_w_down)))
    gd = w_fourier.shape[1] // N_FOURIER_GROUPS
    cos_c, sin_c = _dft_tables(gd, gd ** -0.5)
    cs = jnp.concatenate([cos_c, sin_c], axis=1).astype(BF16)
    final_g = final_norm[None]
    return (_trunk(x_prompt, layers, ffn_w, final_g, cs), _trunk(x_sample, layers, ffn_w, final_g, cs))
```

```python
import functools
import math

import jax
import jax.numpy as jnp
from jax import lax
from jax.experimental import pallas as pl
from jax.experimental.pallas import tpu as pltpu

N_HEADS = 16
QK_NOPE_DIM = 128
QK_ROPE_DIM = 64
V_HEAD_DIM = 128
N_FOURIER_GROUPS = 4
N_BRANCHES = 2
ROPE_THETA = 10000.0
NORM_EPS = 1e-6

LANES = 128
QK_PAD_DIM = 2 * LANES
VT_ROWS = V_HEAD_DIM + 16
ATTN_KEY_CHUNKS = 4
ATTN_PAIRS = 16
FFN_TILE = 512
CAST_ROWS = 256
DFT_ROW_SPLIT = 64
VMEM_LIMIT_BYTES = 56 * 1024 * 1024

F32 = jnp.float32
BF16 = jnp.bfloat16
_NT = (((1,), (1,)), ((), ()))


def _params(*semantics):
    return pltpu.CompilerParams(dimension_semantics=semantics, vmem_limit_bytes=VMEM_LIMIT_BYTES)


def _resident(shape):
    zeros = (0,) * len(shape)
    return pl.BlockSpec(shape, lambda *_: zeros, pipeline_mode=pl.Buffered(1))


def _rms(x):
    return x * lax.rsqrt(jnp.mean(x * x, axis=-1, keepdims=True) + NORM_EPS)


def _dot(a, b):
    return jnp.dot(a, b, preferred_element_type=F32)


def _cast_kernel(w_ref, o_ref):
    o_ref[...] = w_ref[...].astype(BF16)


def _cast_bf16(w, *, tr=CAST_ROWS):
    lead, r, c = w.shape
    rows = lead * r
    tr = min(tr, rows)
    assert rows % tr == 0
    out = pl.pallas_call(
        _cast_kernel,
        out_shape=jax.ShapeDtypeStruct((rows, c), BF16),
        grid=(rows // tr,),
        in_specs=[pl.BlockSpec((tr, c), lambda i: (i, 0))],
        out_specs=pl.BlockSpec((tr, c), lambda i: (i, 0)),
        compiler_params=_params("parallel"),
        name="cast_bf16",
    )(w.reshape(rows, c))
    return out.reshape(lead, r, c)


def _ffn_kernel(x_ref, g_ref, wg_ref, wu_ref, wd_ref, *rest, n_ff, n_split, final):
    if final:
        fg_ref, o_ref, h_ref = rest
    else:
        o_ref, h_ref = rest
    f = pl.program_id(1)

    def half_swiglu(h):
        a = _dot(h, wg_ref[...])
        u = _dot(h, wu_ref[...])
        act = (0.5 * (a * jax.nn.sigmoid(a)) * u).astype(BF16)
        return _dot(act, wd_ref[...])

    @pl.when(f == 0)
    def _():
        rows = x_ref.shape[0] // n_split
        for r in range(n_split):
            sl = slice(r * rows, (r + 1) * rows)
            x = x_ref[sl, :]
            h = (_rms(x) * g_ref[...]).astype(BF16)
            h_ref[sl, :] = h
            o_ref[sl, :] = x + half_swiglu(h)

    @pl.when(f > 0)
    def _():
        o_ref[...] += half_swiglu(h_ref[...])

    if final:
        @pl.when(f == n_ff - 1)
        def _():
            o_ref[...] = _rms(o_ref[...]) * fg_ref[...]


def _ffn(x, g, wg, wu, wd, layer, final_g=None, *, tm=1024, tf=FFN_TILE):
    t, d = x.shape
    d_ff = wg.shape[2]
    tm, tf = min(tm, t), min(tf, d_ff)
    n_ff = d_ff // tf
    assert t % tm == 0 and d_ff % tf == 0
    final = final_g is not None
    in_specs = [
        pl.BlockSpec((tm, d), lambda i, f: (i, 0)),
        _resident((1, d)),
        pl.BlockSpec((None, d, tf), lambda i, f: (layer, 0, f)),
        pl.BlockSpec((None, d, tf), lambda i, f: (layer, 0, f)),
        pl.BlockSpec((None, tf, d), lambda i, f: (layer, f, 0)),
    ]
    args = [x, g, wg, wu, wd]
    if final:
        in_specs.append(_resident((1, d)))
        args.append(final_g)
    return pl.pallas_call(
        functools.partial(_ffn_kernel, n_ff=n_ff, n_split=2 if tm % 512 == 0 else 1, final=final),
        out_shape=jax.ShapeDtypeStruct((t, d), F32),
        grid=(t // tm, n_ff),
        in_specs=in_specs,
        out_specs=pl.BlockSpec((tm, d), lambda i, f: (i, 0)),
        scratch_shapes=[pltpu.VMEM((tm, d), BF16)],
        compiler_params=_params("parallel", "arbitrary"),
        name="ffn_final" if final else "ffn",
    )(*args)


def _fourier_in(h, wf_ref, cs_ref, perm_ref, r_ref, r, *, gd, n_groups):
    u = _dot(h, wf_ref[...]).astype(BF16)
    u = _dot(perm_ref[...], u).astype(BF16)
    half = u.shape[0] // 2
    cs = cs_ref[...]
    for grp in range(n_groups):
        cols = slice(grp * gd, (grp + 1) * gd)
        a = _dot(u[:, cols], cs).astype(BF16)
        for parity in range(2):
            src = slice(parity * half, (parity + 1) * half)
            dst = slice(r * half, (r + 1) * half)
            r_ref[parity, 0, dst, cols] = a[src, :gd]
            r_ref[parity, 1, dst, cols] = a[src, gd:]


def _parity_permutation(rows):
    src = jnp.arange(rows, dtype=jnp.int32)
    src = jnp.where(src < rows // 2, 2 * src, 2 * (src - rows // 2) + 1)
    return (src[:, None] == jnp.arange(rows, dtype=jnp.int32)[None, :]).astype(BF16)


def _seq_dft_kernel(fe_ref, fo_ref, re_ref, ro_ref, o_ref):
    e = _dot(fe_ref[...], re_ref[...])
    o = _dot(fo_ref[...], ro_ref[...])
    o_ref[0] = (e + o).astype(BF16)
    o_ref[1] = (e - o).astype(BF16)


def _seq_dft(fe, fo, r, *, tm=256, tn=1024):
    batch, _, seq, fw = r.shape
    half = seq // 2
    tm, tn = min(tm, half), min(tn, fw)
    assert half % tm == 0 and fw % tn == 0
    out = pl.pallas_call(
        _seq_dft_kernel,
        out_shape=jax.ShapeDtypeStruct((batch, 2, half, fw), BF16),
        grid=(batch, fw // tn, half // tm),
        in_specs=[
            pl.BlockSpec((tm, seq), lambda b, n, i: (i, 0)),
            pl.BlockSpec((tm, seq), lambda b, n, i: (i, 0)),
            pl.BlockSpec((None, None, seq, tn), lambda b, n, i: (b, 0, 0, n)),
            pl.BlockSpec((None, None, seq, tn), lambda b, n, i: (b, 1, 0, n)),
        ],
        out_specs=pl.BlockSpec((None, 2, tm, tn), lambda b, n, i: (b, 0, i, n)),
        compiler_params=_params("parallel", "parallel", "arbitrary"),
        name="seq_dft",
    )(fe, fo, r, r)
    return out.reshape(batch * seq, fw)


def _latent_kernel(x_ref, g_ref, wlat_ref, qg_ref, kvg_ref, wqn_ref, wqr_ref, wqrr_ref, wk_ref, wvt_ref,
                   cos_ref, sin_ref, wf_ref, cs_ref, perm_ref, q_ref, k_ref, vt_ref, r_ref,
                   *, ql, kl, n_heads, dv, scale, n_split, gd, n_groups):
    rope = QK_ROPE_DIM
    rows = x_ref.shape[0] // n_split
    low = lax.broadcasted_iota(jnp.int32, (rows, LANES), 1) < rope

    def low_half(x):
        return jnp.where(low, x, 0.0).astype(BF16)

    ones = jnp.ones((VT_ROWS - dv, rows), BF16)
    for r in range(n_split):
        sl = slice(r * rows, (r + 1) * rows)
        h = (_rms(x_ref[sl, :]) * g_ref[...]).astype(BF16)
        _fourier_in(h, wf_ref, cs_ref, perm_ref, r_ref, r, gd=gd, n_groups=n_groups)
        lat = _dot(h, wlat_ref[...])
        cq = (_rms(lat[:, :ql]) * qg_ref[...]).astype(BF16)
        ckv = (_rms(lat[:, ql:ql + kl]) * kvg_ref[...]).astype(BF16)
        cos = cos_ref[sl, :]
        sin = sin_ref[sl, :]
        kk = lat[:, ql + kl:ql + kl + LANES] * jnp.where(low, cos, sin)
        k_rope = low_half(kk + pltpu.roll(kk, rope, 1))
        qn = _dot(cq, wqn_ref[...]) * scale
        qa = _dot(cq, wqr_ref[...])
        qb = _dot(cq, wqrr_ref[...])
        kn = _dot(ckv, wk_ref[...]).astype(BF16)
        vt = lax.dot_general(wvt_ref[...], ckv, _NT, preferred_element_type=F32).astype(BF16)
        for hd in range(n_heads):
            src = slice(hd * LANES, (hd + 1) * LANES)
            lo = slice(hd * QK_PAD_DIM, hd * QK_PAD_DIM + LANES)
            hi = slice(hd * QK_PAD_DIM + LANES, (hd + 1) * QK_PAD_DIM)
            pair = slice(hd // 2 * LANES, (hd // 2 + 1) * LANES)
            q_rope = (qa[:, pair] * cos + qb[:, pair] * sin) * scale
            q_ref[sl, lo] = qn[:, src].astype(BF16)
            q_ref[sl, hi] = low_half(q_rope if hd % 2 == 0 else pltpu.roll(q_rope, rope, 1))
            k_ref[sl, lo] = kn[:, src]
            k_ref[sl, hi] = k_rope
            vt_ref[hd, :dv, sl] = vt[hd * dv:(hd + 1) * dv, :]
            vt_ref[hd, dv:, sl] = ones


def _latent(x, g, wlat, qg, kvg, wqn, wqr, wqrr, wk, wvt, cos, sin, wf, cs, batch, seq, *, tm=256):
    t, d = x.shape
    ql, kl = qg.shape[1], kvg.shape[1]
    hn = wqn.shape[1]
    n_heads = hn // LANES
    assert n_heads % 2 == 0 and 2 * QK_ROPE_DIM == LANES
    fw = wf.shape[1]
    gd = fw // N_FOURIER_GROUPS
    dv = wvt.shape[0] // n_heads
    tm = min(tm, seq)
    tps = seq // tm
    n_split = 2 if tm % 512 == 0 else 1
    rows = tm // n_split
    assert seq % tm == 0 and rows % 2 == 0
    scale = float(QK_NOPE_DIM + QK_ROPE_DIM) ** -0.5 * math.log2(math.e)
    row = lambda i: (i, 0)
    return pl.pallas_call(
        functools.partial(_latent_kernel, ql=ql, kl=kl, n_heads=n_heads, dv=dv, scale=scale,
                          n_split=n_split, gd=gd, n_groups=N_FOURIER_GROUPS),
        out_shape=(
            jax.ShapeDtypeStruct((t, n_heads * QK_PAD_DIM), BF16),
            jax.ShapeDtypeStruct((t, n_heads * QK_PAD_DIM), BF16),
            jax.ShapeDtypeStruct((batch, n_heads, VT_ROWS, seq), BF16),
            jax.ShapeDtypeStruct((batch, 2, 2, seq // 2, fw), BF16),
        ),
        grid=(t // tm,),
        in_specs=[
            pl.BlockSpec((tm, d), row),
            _resident((1, d)),
            _resident(wlat.shape),
            _resident((1, ql)),
            _resident((1, kl)),
            _resident(wqn.shape),
            _resident(wqr.shape),
            _resident(wqrr.shape),
            _resident(wk.shape),
            _resident(wvt.shape),
            pl.BlockSpec((tm, LANES), lambda i: (i % tps, 0)),
            pl.BlockSpec((tm, LANES), lambda i: (i % tps, 0)),
            _resident(wf.shape),
            _resident((gd, 2 * gd)),
            _resident((rows, rows)),
        ],
        out_specs=(
            pl.BlockSpec((tm, n_heads * QK_PAD_DIM), row),
            pl.BlockSpec((tm, n_heads * QK_PAD_DIM), row),
            pl.BlockSpec((None, n_heads, VT_ROWS, tm), lambda i: (i // tps, 0, 0, i % tps)),
            pl.BlockSpec((None, 2, 2, tm // 2, fw), lambda i: (i // tps, 0, 0, i % tps, 0)),
        ),
        compiler_params=_params("parallel"),
        name="latent",
    )(x, g, wlat, qg, kvg, wqn, wqr, wqrr, wk, wvt, cos, sin, wf, cs, _parity_permutation(rows))


def _attn_kernel(q_ref, k_ref, vt_ref, o_ref, st_ref, pt_ref, *, n_heads, n_sub, tsub, kc, dv):
    n_chunks = k_ref.shape[0] // kc
    pairs = [(h, j) for h in range(n_heads) for j in range(n_sub)]
    col_max = [None] * len(pairs)
    for t in range(-1, len(pairs) + 1):
        run_max = None
        ot = None
        for c in range(n_chunks):
            rows = slice(c * kc, (c + 1) * kc)
            if t + 1 < len(pairs):
                h, j = pairs[t + 1]
                head = slice(h * QK_PAD_DIM, (h + 1) * QK_PAD_DIM)
                s = lax.dot_general(k_ref[rows, head], q_ref[j * tsub:(j + 1) * tsub, head], _NT,
                                    preferred_element_type=F32)
                st_ref[(t + 1) % 2, rows, :] = s
                chunk_max = jnp.max(s, axis=0, keepdims=True)
                run_max = chunk_max if run_max is None else jnp.maximum(run_max, chunk_max)
            if 0 <= t < len(pairs):
                pt_ref[t % 2, rows, :] = jnp.exp2(st_ref[t % 2, rows, :] - col_max[t]).astype(BF16)
            if t >= 1:
                h, j = pairs[t - 1]
                part = _dot(vt_ref[h, :, rows], pt_ref[(t - 1) % 2, rows, :])
                ot = part if ot is None else ot + part
        if t + 1 < len(pairs):
            col_max[t + 1] = run_max
        if t >= 1:
            h, j = pairs[t - 1]
            o = ot[:dv] / ot[dv:dv + 1]
            o_ref[j * tsub:(j + 1) * tsub, h * dv:(h + 1) * dv] = o.T.astype(BF16)


def _attention(q, k, vt, batch, seq, *, tq=4096, tsub=512, kc=None):
    t = q.shape[0]
    n_heads = q.shape[1] // QK_PAD_DIM
    dv = V_HEAD_DIM
    tq = min(tq, seq)
    tsub = min(tsub, tq)
    n_sub = tq // tsub
    hb = max(1, ATTN_PAIRS // n_sub)
    hb = hb if n_heads % hb == 0 else 1
    kc = seq // ATTN_KEY_CHUNKS if kc is None else min(kc, seq)
    tps = seq // tq
    assert seq % tq == 0 and tq % tsub == 0 and seq % kc == 0
    return pl.pallas_call(
        functools.partial(_attn_kernel, n_heads=hb, n_sub=n_sub, tsub=tsub, kc=kc, dv=dv),
        out_shape=jax.ShapeDtypeStruct((t, n_heads * dv), BF16),
        grid=(batch, n_heads // hb, tps),
        in_specs=[
            pl.BlockSpec((tq, hb * QK_PAD_DIM), lambda b, h, i: (b * tps + i, h)),
            pl.BlockSpec((seq, hb * QK_PAD_DIM), lambda b, h, i: (b, h)),
            pl.BlockSpec((None, hb, VT_ROWS, seq), lambda b, h, i: (b, h, 0, 0)),
        ],
        out_specs=pl.BlockSpec((tq, hb * dv), lambda b, h, i: (b * tps + i, h)),
        scratch_shapes=[pltpu.VMEM((2, seq, tsub), F32), pltpu.VMEM((2, seq, tsub), BF16)],
        compiler_params=_params("parallel", "parallel", "arbitrary"),
        name="attention",
    )(q, k, vt)


def _gate_kernel(x_ref, g_ref, w_ref, b_ref, o_ref, *, n_split):
    rows = x_ref.shape[0] // n_split
    for r in range(n_split):
        sl = slice(r * rows, (r + 1) * rows)
        h = (_rms(x_ref[sl, :]) * g_ref[...]).astype(BF16)
        o_ref[sl, :] = jax.nn.sigmoid(_dot(h, w_ref[...]) + b_ref[...]).astype(BF16)


def _gates(x, g, w, b, *, tm=512):
    t, d = x.shape
    n = w.shape[1]
    tm = min(tm, t)
    assert t % tm == 0
    return pl.pallas_call(
        functools.partial(_gate_kernel, n_split=2 if tm % 512 == 0 else 1),
        out_shape=jax.ShapeDtypeStruct((t, n), BF16),
        grid=(t // tm,),
        in_specs=[
            pl.BlockSpec((tm, d), lambda i: (i, 0)),
            _resident((1, d)),
            _resident((d, n)),
            _resident((1, n)),
        ],
        out_specs=pl.BlockSpec((tm, n), lambda i: (i, 0)),
        compiler_params=_params("parallel"),
        name="gates",
    )(x, g, w, b)


def _merge_kernel(x_ref, mix_ref, att_ref, gate_ref, wf_ref, wo_ref, wout_ref, o_ref, *, d):
    ya = _dot(mix_ref[...], wf_ref[...])
    yb = _dot(att_ref[...], wo_ref[...])
    m = gate_ref[:, :d].astype(F32) * ya + gate_ref[:, d:].astype(F32) * yb
    o_ref[...] = x_ref[...] + _dot(m.astype(BF16), wout_ref[...])


def _merge(x, mixed, att, gates, wf, wo, wout, *, tm=256):
    t, d = x.shape
    tm = min(tm, t)
    assert t % tm == 0
    row = lambda i: (i, 0)
    return pl.pallas_call(
        functools.partial(_merge_kernel, d=d),
        out_shape=jax.ShapeDtypeStruct((t, d), F32),
        grid=(t // tm,),
        in_specs=[
            pl.BlockSpec((tm, d), row),
            pl.BlockSpec((tm, mixed.shape[1]), row),
            pl.BlockSpec((tm, att.shape[1]), row),
            pl.BlockSpec((tm, gates.shape[1]), row),
            _resident(wf.shape),
            _resident(wo.shape),
            _resident(wout.shape),
        ],
        out_specs=pl.BlockSpec((tm, d), row),
        compiler_params=_params("parallel"),
        name="merge",
    )(x, mixed, att, gates, wf, wo, wout)


def _dft_tables(n, scale):
    def trig(rows, k):
        ang = ((rows[:, None] * k[None, :]) % n).astype(F32) * (2.0 * math.pi / n)
        return jnp.cos(ang), jnp.sin(ang)

    k = jnp.arange(n, dtype=jnp.int32)
    m = DFT_ROW_SPLIT
    if n % m or n <= m:
        c, s = trig(k, k)
        return c * scale, s * scale
    ca, sa = trig(jnp.arange(n // m, dtype=jnp.int32) * m, k)
    cb, sb = trig(jnp.arange(m, dtype=jnp.int32), k)
    cb, sb = cb * scale, sb * scale
    c = ca[:, None, :] * cb[None, :, :] - sa[:, None, :] * sb[None, :, :]
    s = sa[:, None, :] * cb[None, :, :] + ca[:, None, :] * sb[None, :, :]
    return c.reshape(n, n), s.reshape(n, n)


def _seq_dft_matrices(seq):
    half = seq // 2
    c, s = _dft_tables(half, seq ** -0.5)
    phi = jnp.arange(half, dtype=F32)[:, None] * (2.0 * math.pi / seq)
    co, so = c * jnp.cos(phi) - s * jnp.sin(phi), s * jnp.cos(phi) + c * jnp.sin(phi)
    fe = jnp.concatenate([c, -s], axis=1).astype(BF16)
    fo = jnp.concatenate([co, -so], axis=1).astype(BF16)
    return fe, fo


def _rope_tables(seq):
    half = QK_ROPE_DIM // 2
    inv_freq = 1.0 / (ROPE_THETA ** (jnp.arange(half, dtype=F32) / half))
    ang = jnp.arange(seq, dtype=F32)[:, None] * inv_freq[None, :]
    reps = LANES // half
    return jnp.tile(jnp.cos(ang), (1, reps)), jnp.tile(jnp.sin(ang), (1, reps))


def _rot_cols(w):
    half = QK_ROPE_DIM // 2
    return jnp.concatenate([-w[..., half:], w[..., :half]], axis=-1)


def _prep_layer(l, p):
    ql, kl = p["q_a_norm"].shape[1], p["kv_a_norm"].shape[1]
    fw = p["w_fourier"].shape[1]
    w_in = p["w_in"][l]
    s0, s1, s2, s3 = fw, fw + ql, fw + ql + kl, fw + ql + kl + QK_ROPE_DIM
    w_kr = w_in[:, s2:s3]
    wlat = jnp.concatenate([w_in[:, s0:s2], w_kr, _rot_cols(w_kr)], axis=1)
    w_uq = p["w_uq"][l].reshape(ql, N_HEADS, QK_NOPE_DIM + QK_ROPE_DIM)
    w_q_rope = w_uq[:, :, QK_NOPE_DIM:]
    w_ukv = p["w_ukv"][l].reshape(kl, N_HEADS, QK_NOPE_DIM + V_HEAD_DIM)
    bf = lambda a: a.astype(BF16)
    return dict(
        ffn1_norm=p["ffn1_norm"][l][None],
        ffn2_norm=p["ffn2_norm"][l][None],
        mix_norm=p["mix_norm"][l][None],
        wf_in=bf(w_in[:, :s0]),
        wlat=bf(wlat),
        w_gate=bf(w_in[:, s3:]),
        b_gate=p["b_gate"][l][None],
        qg=p["q_a_norm"][l][None],
        kvg=p["kv_a_norm"][l][None],
        wqn=bf(w_uq[:, :, :QK_NOPE_DIM].reshape(ql, N_HEADS * QK_NOPE_DIM)),
        wqr=bf(w_q_rope.reshape(ql, N_HEADS * QK_ROPE_DIM)),
        wqrr=bf(_rot_cols(w_q_rope).reshape(ql, N_HEADS * QK_ROPE_DIM)),
        wk=bf(w_ukv[:, :, :QK_NOPE_DIM].reshape(kl, N_HEADS * QK_NOPE_DIM)),
        wvt=bf(w_ukv[:, :, QK_NOPE_DIM:].reshape(kl, N_HEADS * V_HEAD_DIM).T),
        w_fourier=bf(p["w_fourier"][l]),
        w_mla_o=bf(p["w_mla_o"][l]),
        w_out=bf(p["w_out"][l]),
    )


def _trunk(x3, layers, ffn_w, final_g, cs):
    batch, seq, d = x3.shape
    x = x3.reshape(batch * seq, d)
    fe, fo = _seq_dft_matrices(seq)
    rope_cos, rope_sin = _rope_tables(seq)
    for li, w in enumerate(layers):
        x = _ffn(x, w["ffn1_norm"], *ffn_w["ffn1"], li)
        q, k, vt, r = _latent(x, w["mix_norm"], w["wlat"], w["qg"], w["kvg"], w["wqn"], w["wqr"], w["wqrr"],
                              w["wk"], w["wvt"], rope_cos, rope_sin, w["wf_in"], cs, batch, seq)
        gates = _gates(x, w["mix_norm"], w["w_gate"], w["b_gate"])
        mixed = _seq_dft(fe, fo, r.reshape(batch, 2, seq, r.shape[-1]))
        att = _attention(q, k, vt, batch, seq)
        x = _merge(x, mixed, att, gates, w["w_fourier"], w["w_mla_o"], w["w_out"])
        x = _ffn(x, w["ffn2_norm"], *ffn_w["ffn2"], li, final_g=final_g if li == len(layers) - 1 else None)
    return x.reshape(batch, seq, d)


def kernel(x_prompt, x_sample, ffn1_norm, ffn1_w_gate, ffn1_w_up, ffn1_w_down, mix_norm, w_in, b_gate, q_a_norm, kv_a_norm, w_uq, w_ukv, w_fourier, w_mla_o, w_out, ffn2_norm, ffn2_w_gate, ffn2_w_up, ffn2_w_down, final_norm):
    p = dict(ffn1_norm=ffn1_norm, ffn1_w_gate=ffn1_w_gate, ffn1_w_up=ffn1_w_up, ffn1_w_down=ffn1_w_down,
             mix_norm=mix_norm, w_in=w_in, b_gate=b_gate, q_a_norm=q_a_norm, kv_a_norm=kv_a_norm, w_uq=w_uq,
             w_ukv=w_ukv, w_fourier=w_fourier, w_mla_o=w_mla_o, w_out=w_out, ffn2_norm=ffn2_norm,
             ffn2_w_gate=ffn2_w_gate, ffn2_w_up=ffn2_w_up, ffn2_w_down=ffn2_w_down)
    layers = [_prep_layer(l, p) for l in range(w_in.shape[0])]
    ffn_w = dict(ffn1=tuple(_cast_bf16(w) for w in (ffn1_w_gate, ffn1_w_up, ffn1_w_down)),
                 ffn2=tuple(_cast_bf16(w) for w in (ffn2_w_gate, ffn2_w_up, ffn2_w_down)))
    gd = w_fourier.shape[1] // N_FOURIER_GROUPS
    cos_c, sin_c = _dft_tables(gd, gd ** -0.5)
    cs = jnp.concatenate([cos_c, sin_c], axis=1).astype(BF16)
    final_g = final_norm[None]
    return (_trunk(x_prompt, layers, ffn_w, final_g, cs), _trunk(x_sample, layers, ffn_w, final_g, cs))
```
